```python
import jax, jax.numpy as jnp
from jax import lax
import numpy as np

D_MODEL = 1024
BATCH = 8
SEQ = 2048
DEPTH = 1
DEC_BATCH = 128
DEC_SEQ = 8
PAST_LEN = 16384
PAGE_SIZE = 128

MIX_WIDTH = D_MODEL
RET_HEADS = 8
RET_DK = 64
RET_DV = MIX_WIDTH // 2 // RET_HEADS
RWKV_HEADS = 8
RWKV_N = MIX_WIDTH // 2 // RWKV_HEADS
RET_W = RET_HEADS * RET_DV
RWKV_W = RWKV_HEADS * RWKV_N
DECAY_LORA = 64
AAA_LORA = 64
GATE_LORA = 128
RET_PROJ = 2 * RET_HEADS * RET_DK + 2 * RET_W
RWKV_PROJ = 3 * RWKV_W + DECAY_LORA + AAA_LORA + GATE_LORA
IN_PROJ = RET_PROJ + RWKV_PROJ
RET_CHUNK = 128
ROPE_BASE = 10000.0
PEER_HEADS = 8
PEER_NKEYS = 128
PEER_EXPERTS = PEER_NKEYS * PEER_NKEYS
PEER_DKEY = 256
PEER_TOPK = 16
PEER_BLOCK = 256
PLE_DIM = 256
RMS_EPS = 1e-6
GN_EPS = 1e-5
RWKV_LN_EPS = 64e-5

kernel_name = 'hymba_retnet_rwkv7_peer_step'


def rmsnorm(x, g):
    xf = x.astype(jnp.float32)
    y = xf * lax.rsqrt(jnp.mean(xf * xf, axis=-1, keepdims=True) + RMS_EPS)
    return (y * g.astype(jnp.float32)).astype(x.dtype)


def head_norm(x, eps):
    xf = x.astype(jnp.float32)
    xc = xf - jnp.mean(xf, axis=-1, keepdims=True)
    return xc * lax.rsqrt(jnp.mean(xc * xc, axis=-1, keepdims=True) + eps)


def rope(x, pos):
    d = x.shape[-1]
    half = d // 2
    inv = ROPE_BASE ** (-jnp.arange(half, dtype=jnp.float32) / half)
    ang = pos.astype(jnp.float32)[:, None] * inv[None, :]
    cos = jnp.cos(ang)[None, :, None, :]
    sin = jnp.sin(ang)[None, :, None, :]
    xf = x.astype(jnp.float32)
    x1, x2 = xf[..., :half], xf[..., half:]
    return jnp.concatenate([x1 * cos - x2 * sin, x2 * cos + x1 * sin], axis=-1)


def retention_chunkwise(q, k, v, s0):
    B, L, H, DK = q.shape
    DV = v.shape[-1]
    C = RET_CHUNK if L % RET_CHUNK == 0 else L
    n = L // C
    lg = jnp.log(1.0 - 2.0 ** (-5.0 - jnp.arange(H, dtype=jnp.float32)))
    idx = jnp.arange(C, dtype=jnp.float32)
    diff = idx[:, None] - idx[None, :]
    dmat = jnp.where(diff[None] >= 0, jnp.exp(jnp.maximum(diff, 0.0)[None] * lg[:, None, None]), 0.0)
    xi = jnp.exp((idx + 1.0)[None, :] * lg[:, None])
    zeta = jnp.exp((C - 1.0 - idx)[None, :] * lg[:, None])
    g_chunk = jnp.exp(C * lg)

    def to_chunks(t):
        return t.astype(jnp.float32).reshape(B, n, C, H, t.shape[-1]).transpose(1, 0, 3, 2, 4)

    def step(S, xs):
        qc, kc, vc = xs
        inner = jnp.einsum('bhid,bhjd->bhij', qc, kc) * dmat
        o = jnp.einsum('bhij,bhjv->bhiv', inner, vc) + jnp.einsum('bhid,bhdv->bhiv', qc, S) * xi[:, :, None]
        S = S * g_chunk[:, None, None] + jnp.einsum('bhjd,bhjv->bhdv', kc * zeta[:, :, None], vc)
        return S, o

    S, o = lax.scan(step, s0.astype(jnp.float32), (to_chunks(q), to_chunks(k), to_chunks(v)))
    o = o.transpose(1, 0, 3, 2, 4).reshape(B, L, H, DV)
    return S, o


def rwkv7_scan(r, log_w, k, kk, a, v, s0):
    def tm(t):
        return t.astype(jnp.float32).transpose(1, 0, 2, 3)

    def step(S, xs):
        rt, lwt, kt, kkt, at, vt = xs
        sa = jnp.einsum('bhij,bhj->bhi', S, -kkt)
        S = (S * jnp.exp(lwt)[:, :, None, :] + sa[..., None] * (kkt * at)[:, :, None, :]
             + vt[..., None] * kt[:, :, None, :])
        y = jnp.einsum('bhij,bhj->bhi', S, rt)
        return S, y

    S, y = lax.scan(step, s0.astype(jnp.float32), (tm(r), tm(log_w), tm(k), tm(kk), tm(a), tm(v)))
    return S, y.transpose(1, 0, 2, 3)


def token_mixers(xn, pos, s_ret, s_wkv, s_shift, w_in, rw_mu, rw_w0, rw_w2, rw_a0, rw_a2, rw_g2,
                 rw_kk, rw_ka, rw_rk, rw_ln_g, rw_ln_b, w_out):
    B, L, _ = xn.shape
    proj = xn @ w_in
    ret_p, rw_p = proj[..., :RET_PROJ], proj[..., RET_PROJ:]
    qk = RET_HEADS * RET_DK
    q = rope(ret_p[..., :qk].reshape(B, L, RET_HEADS, RET_DK), pos)
    k = rope(ret_p[..., qk:2 * qk].reshape(B, L, RET_HEADS, RET_DK), pos) * (RET_DK ** -0.5)
    v = ret_p[..., 2 * qk:2 * qk + RET_W].reshape(B, L, RET_HEADS, RET_DV)
    g_ret = ret_p[..., 2 * qk + RET_W:]
    s_ret_new, o_ret = retention_chunkwise(q, k, v, s_ret)
    o_ret = (head_norm(o_ret, GN_EPS).reshape(B, L, RET_W) * jax.nn.silu(g_ret.astype(jnp.float32))).astype(xn.dtype)
    prev = jnp.concatenate([s_shift[:, None, :].astype(rw_p.dtype), rw_p[:, :-1]], axis=1)
    f = rw_p + (prev - rw_p) * rw_mu
    new_shift = rw_p[:, -1]
    o1, o2, o3 = RWKV_W, 2 * RWKV_W, 3 * RWKV_W
    r = f[..., :o1]
    k7 = f[..., o1:o2]
    v7 = f[..., o2:o3]
    fw = f[..., o3:o3 + DECAY_LORA].astype(jnp.float32)
    fa = f[..., o3 + DECAY_LORA:o3 + DECAY_LORA + AAA_LORA].astype(jnp.float32)
    fg = f[..., o3 + DECAY_LORA + AAA_LORA:].astype(jnp.float32)
    w = -jax.nn.softplus(-(rw_w0 + jnp.tanh(fw) @ rw_w2)) - 0.5
    log_w = -jnp.exp(w)
    a = jax.nn.sigmoid(rw_a0 + fa @ rw_a2)
    gate = jax.nn.sigmoid(fg) @ rw_g2
    kk = (k7 * rw_kk).astype(jnp.float32).reshape(B, L, RWKV_HEADS, RWKV_N)
    kk = kk / jnp.maximum(jnp.sqrt(jnp.sum(kk * kk, axis=-1, keepdims=True)), 1e-12)
    k7 = k7.astype(jnp.float32) * (1.0 + (a - 1.0) * rw_ka)
    hs = (B, L, RWKV_HEADS, RWKV_N)
    r4, k4, v4, a4 = r.astype(jnp.float32).reshape(hs), k7.reshape(hs), v7.astype(jnp.float32).reshape(hs), a.reshape(hs)
    s_wkv_new, o_rw = rwkv7_scan(r4, log_w.reshape(hs), k4, kk, a4, v4, s_wkv)
    o_rw = head_norm(o_rw, RWKV_LN_EPS) * rw_ln_g.reshape(RWKV_HEADS, RWKV_N) + rw_ln_b.reshape(RWKV_HEADS, RWKV_N)
    o_rw = o_rw + jnp.sum(r4 * k4 * rw_rk, axis=-1, keepdims=True) * v4
    o_rw = (o_rw.reshape(B, L, RWKV_W) * gate).astype(xn.dtype)
    out = jnp.concatenate([o_ret, o_rw], axis=-1) @ w_out
    return out, s_ret_new, s_wkv_new, new_shift


def peer(xn, wq, keys, u_tab, v_tab):
    B, L, D = xn.shape
    T = B * L
    nb = -(-T // PEER_BLOCK)
    xt = jnp.pad(xn.reshape(T, D), ((0, nb * PEER_BLOCK - T), (0, 0))).reshape(nb, PEER_BLOCK, D)

    def block(xb):
        q = (xb @ wq).reshape(PEER_BLOCK, PEER_HEADS, 2, PEER_DKEY // 2)
        s = jnp.einsum('phcd,hcnd->phcn', q, keys).astype(jnp.float32)
        sv, si = lax.top_k(s, PEER_TOPK)
        cand = sv[:, :, 0, :, None] + sv[:, :, 1, None, :]
        cv, ci = lax.top_k(cand.reshape(PEER_BLOCK, PEER_HEADS, PEER_TOPK * PEER_TOPK), PEER_TOPK)
        e = (jnp.take_along_axis(si[:, :, 0], ci // PEER_TOPK, axis=-1) * PEER_NKEYS
             + jnp.take_along_axis(si[:, :, 1], ci % PEER_TOPK, axis=-1))
        gsm = jax.nn.softmax(cv, axis=-1)
        hval = jnp.einsum('pd,phkd->phk', xb, u_tab[e])
        act = (jax.nn.gelu(hval.astype(jnp.float32), approximate=False) * gsm).astype(xb.dtype)
        return jnp.einsum('phk,phkd->pd', act, v_tab[e])

    y = lax.map(block, xt).reshape(nb * PEER_BLOCK, D)[:T]
    return y.reshape(B, L, D)


def layer(h, p_l, pos, s_ret, s_wkv, s_shift, lp):
    (norm_mix, w_in, rw_mu, rw_w0, rw_w2, rw_a0, rw_a2, rw_g2, rw_kk, rw_ka, rw_rk, rw_ln_g, rw_ln_b,
     w_out, norm_ffn, peer_wq, peer_keys, peer_u, peer_v, ple_norm, ple_gate, ple_proj) = lp
    xn = rmsnorm(h, norm_mix)
    mix, s_ret_new, s_wkv_new, new_shift = token_mixers(
        xn, pos, s_ret, s_wkv, s_shift, w_in, rw_mu, rw_w0, rw_w2, rw_a0, rw_a2, rw_g2,
        rw_kk, rw_ka, rw_rk, rw_ln_g, rw_ln_b, w_out)
    h = h + mix
    h = h + peer(rmsnorm(h, norm_ffn), peer_wq, peer_keys, peer_u, peer_v)
    gate = jax.nn.sigmoid((rmsnorm(h, ple_norm) @ ple_gate).astype(jnp.float32))
    h = h + (gate * (p_l @ ple_proj)).astype(h.dtype)
    return h, s_ret_new, s_wkv_new, new_shift


def setup_inputs(seed: int = 0) -> dict:
    key = jax.random.key(seed)
    ks = jax.random.split(key, 40)
    f32 = jnp.float32

    def nrm(k, shape, scale):
        return jax.random.normal(k, shape, f32) * scale

    return {
        'x_prompt': nrm(ks[0], (BATCH, SEQ, D_MODEL), 1.0),
        'x_sample': nrm(ks[1], (DEC_BATCH, DEC_SEQ, D_MODEL), 1.0),
        'state_ret': nrm(ks[2], (DEPTH, DEC_BATCH, RET_HEADS, RET_DK, RET_DV), 0.5),
        'state_wkv': nrm(ks[3], (DEPTH, DEC_BATCH, RWKV_HEADS, RWKV_N, RWKV_N), 0.5),
        'state_shift': nrm(ks[4], (DEPTH, DEC_BATCH, RWKV_PROJ), 1.0),
        'p_prompt': nrm(ks[5], (DEPTH, BATCH, SEQ, PLE_DIM), 1.0),
        'p_sample': nrm(ks[6], (DEPTH, DEC_BATCH, DEC_SEQ, PLE_DIM), 1.0),
        'norm_mix': 1.0 + nrm(ks[7], (DEPTH, D_MODEL), 0.05),
        'w_in': nrm(ks[8], (DEPTH, D_MODEL, IN_PROJ), D_MODEL ** -0.5),
        'rw_mu': jax.random.uniform(ks[9], (DEPTH, RWKV_PROJ), f32, 0.0, 1.0),
        'rw_w0': jax.random.uniform(ks[10], (DEPTH, RWKV_W), f32, -5.0, 0.0),
        'rw_w2': nrm(ks[11], (DEPTH, DECAY_LORA, RWKV_W), 0.1),
        'rw_a0': nrm(ks[12], (DEPTH, RWKV_W), 0.5),
        'rw_a2': nrm(ks[13], (DEPTH, AAA_LORA, RWKV_W), 0.5 * AAA_LORA ** -0.5),
        'rw_g2': nrm(ks[14], (DEPTH, GATE_LORA, RWKV_W), GATE_LORA ** -0.5),
        'rw_kk': 0.85 + nrm(ks[15], (DEPTH, RWKV_W), 0.05),
        'rw_ka': 1.0 + nrm(ks[16], (DEPTH, RWKV_W), 0.05),
        'rw_rk': nrm(ks[17], (DEPTH, RWKV_HEADS, RWKV_N), 0.1),
        'rw_ln_g': 1.0 + nrm(ks[18], (DEPTH, RWKV_W), 0.05),
        'rw_ln_b': nrm(ks[19], (DEPTH, RWKV_W), 0.01),
        'w_out': nrm(ks[20], (DEPTH, MIX_WIDTH, D_MODEL), MIX_WIDTH ** -0.5),
        'norm_ffn': 1.0 + nrm(ks[21], (DEPTH, D_MODEL), 0.05),
        'peer_wq': nrm(ks[22], (DEPTH, D_MODEL, PEER_HEADS * PEER_DKEY), D_MODEL ** -0.5),
        'peer_keys': nrm(ks[23], (DEPTH, PEER_HEADS, 2, PEER_NKEYS, PEER_DKEY // 2), (PEER_DKEY // 2) ** -0.5),
        'peer_u': nrm(ks[24], (DEPTH, PEER_EXPERTS, D_MODEL), D_MODEL ** -0.5),
        'peer_v': nrm(ks[25], (DEPTH, PEER_EXPERTS, D_MODEL), 0.5 * PEER_HEADS ** -0.5),
        'ple_norm': 1.0 + nrm(ks[26], (DEPTH, D_MODEL), 0.05),
        'ple_gate': nrm(ks[27], (DEPTH, D_MODEL, D_MODEL), D_MODEL ** -0.5),
        'ple_proj': nrm(ks[28], (DEPTH, PLE_DIM, D_MODEL), PLE_DIM ** -0.5),
        'norm_final': 1.0 + nrm(ks[29], (D_MODEL,), 0.05),
    }


def reference(x_prompt, x_sample, state_ret, state_wkv, state_shift, p_prompt, p_sample,
              norm_mix, w_in, rw_mu, rw_w0, rw_w2, rw_a0, rw_a2, rw_g2, rw_kk, rw_ka, rw_rk,
              rw_ln_g, rw_ln_b, w_out, norm_ffn, peer_wq, peer_keys, peer_u, peer_v,
              ple_norm, ple_gate, ple_proj, norm_final):
    pos_p = jnp.arange(SEQ, dtype=jnp.int32)
    pos_s = PAST_LEN + jnp.arange(DEC_SEQ, dtype=jnp.int32)
    zero_ret = jnp.zeros((BATCH, RET_HEADS, RET_DK, RET_DV), jnp.float32)
    zero_wkv = jnp.zeros((BATCH, RWKV_HEADS, RWKV_N, RWKV_N), jnp.float32)
    zero_shift = jnp.zeros((BATCH, RWKV_PROJ), x_prompt.dtype)
    hp, hs = x_prompt, x_sample
    rp_l, wp_l, sp_l, rs_l, ws_l, ss_l = [], [], [], [], [], []
    for i in range(DEPTH):
        lp = (norm_mix[i], w_in[i], rw_mu[i], rw_w0[i], rw_w2[i], rw_a0[i], rw_a2[i], rw_g2[i],
              rw_kk[i], rw_ka[i], rw_rk[i], rw_ln_g[i], rw_ln_b[i], w_out[i], norm_ffn[i],
              peer_wq[i], peer_keys[i], peer_u[i], peer_v[i], ple_norm[i], ple_gate[i], ple_proj[i])
        hp, rp, wp, sp = layer(hp, p_prompt[i], pos_p, zero_ret, zero_wkv, zero_shift, lp)
        hs, rs, ws, ss = layer(hs, p_sample[i], pos_s, state_ret[i], state_wkv[i], state_shift[i], lp)
        rp_l.append(rp); wp_l.append(wp); sp_l.append(sp)
        rs_l.append(rs); ws_l.append(ws); ss_l.append(ss)
    y_prompt = rmsnorm(hp, norm_final)
    y_sample = rmsnorm(hs, norm_final)
    ret_prompt = jnp.stack(rp_l)
    wkv_prompt = jnp.stack(wp_l)
    shift_prompt = jnp.stack(sp_l)
    ret_sample = jnp.stack(rs_l)
    wkv_sample = jnp.stack(ws_l)
    shift_sample = jnp.stack(ss_l)
    return (y_prompt, y_sample, ret_prompt, wkv_prompt, shift_prompt, ret_sample, wkv_sample, shift_sample)
```

```python
import functools
import math

import jax
import jax.numpy as jnp
from jax import lax
from jax.experimental import pallas as pl
from jax.experimental.pallas import tpu as pltpu

F32 = jnp.float32
BF16 = jnp.bfloat16
HIGHEST = lax.Precision.HIGHEST

D_MODEL = 1024
PAST_LEN = 16384
RET_HEADS = 8
RET_DK = 64
RET_DV = 64
RET_W = RET_HEADS * RET_DV
RWKV_HEADS = 8
RWKV_N = 64
RWKV_W = RWKV_HEADS * RWKV_N
DECAY_LORA = 64
AAA_LORA = 64
GATE_LORA = 128
RET_PROJ = 2 * RET_HEADS * RET_DK + 2 * RET_W
RWKV_PROJ = 3 * RWKV_W + DECAY_LORA + AAA_LORA + GATE_LORA
RET_CHUNK = 128
ROPE_BASE = 10000.0
PEER_HEADS = 8
PEER_NKEYS = 128
PEER_EXPERTS = PEER_NKEYS * PEER_NKEYS
PEER_DKEY = 256
PEER_TOPK = 16
PLE_DIM = 256
RMS_EPS = 1e-6
GN_EPS = 1e-5
RWKV_LN_EPS = 64e-5

LANES = 128
SUBLANES = 8
BF16_ROWS = 16
VMEM_LIMIT = 56 * 1024 * 1024

NEG_INF = float("-inf")


def _cparams(*sem):
    return pltpu.CompilerParams(dimension_semantics=sem, vmem_limit_bytes=VMEM_LIMIT)


def _rms(x, g):
    return x * lax.rsqrt(jnp.mean(x * x, axis=-1, keepdims=True) + RMS_EPS) * g


def _split(x):
    hi = x.astype(BF16)
    return hi, (x - hi.astype(F32)).astype(BF16)


def _dot_nt(a, b, precision=None):
    return lax.dot_general(a, b, (((1,), (1,)), ((), ())), precision=precision,
                           preferred_element_type=F32)


def _dot_tn(a, b, precision=None):
    return lax.dot_general(a, b, (((0,), (0,)), ((), ())), precision=precision,
                           preferred_element_type=F32)


PEER_TT = 512
PEER_ROWS = 8
PEER_EB = PEER_ROWS * PEER_NKEYS
_CANDS = tuple((a, b) for a in range(PEER_TOPK) for b in range(PEER_TOPK)
               if (a + 1) * (b + 1) <= PEER_TOPK)


def _peer_kernel(h_ref, g_ref, wqt_ref, keys_ref, u_ref, vt_ref, o_ref,
                 xn_ref, s_ref, e_ref, top_ref, tau_ref, acc_ref, act_ref, *, tt):
    eb = pl.program_id(1)
    nlb = tt // LANES

    @pl.when(eb == 0)
    def _prepare():
        xn = _rms(h_ref[...], g_ref[...])
        xh, xl = _split(xn)
        xn_ref[...] = xh
        qt = (_dot_nt(wqt_ref[0], xh) + _dot_nt(wqt_ref[0], xl) + _dot_nt(wqt_ref[1], xh))
        for hc in range(2 * PEER_HEADS):
            qh, ql = _split(qt[hc * PEER_NKEYS:(hc + 1) * PEER_NKEYS, :])
            s_ref[hc] = (jnp.dot(keys_ref[0, hc], qh, preferred_element_type=F32)
                         + jnp.dot(keys_ref[0, hc], ql, preferred_element_type=F32)
                         + jnp.dot(keys_ref[1, hc], qh, preferred_element_type=F32))

        def per_lane_block(lb, carry):
            lanes = pl.ds(pl.multiple_of(lb * LANES, LANES), LANES)
            for hc in range(2 * PEER_HEADS):
                h, c = divmod(hc, 2)
                work = s_ref[hc, :, lanes]
                e_ref[hc, :, lanes] = jnp.exp(work - jnp.max(work, axis=0, keepdims=True))
                for r in range(PEER_TOPK):
                    m = jnp.max(work, axis=0, keepdims=True)
                    top_ref[c, r, pl.ds(h, 1), lanes] = m
                    work = jnp.where(work == m, NEG_INF, work)
            v1 = [top_ref[0, r, :, lanes] for r in range(PEER_TOPK)]
            v2 = [top_ref[1, r, :, lanes] for r in range(PEER_TOPK)]
            cands = [v1[a] + v2[b] for (a, b) in _CANDS]
            cv = []
            for r in range(PEER_TOPK):
                m = functools.reduce(jnp.maximum, cands)
                cv.append(m)
                if r + 1 < PEER_TOPK:
                    cands = [jnp.where(x == m, NEG_INF, x) for x in cands]
            z = functools.reduce(lambda a, b: a + b, [jnp.exp(x - cv[0]) for x in cv])
            tau_ref[0, :, lanes] = cv[PEER_TOPK - 1]
            tau_ref[1, :, lanes] = 1.0 / z
            return carry

        lax.fori_loop(0, nlb, per_lane_block, 0)
        acc_ref[...] = jnp.zeros_like(acc_ref)

    ht = _dot_nt(u_ref[...], xn_ref[...])
    act_ref[...] = (0.5 * ht * (1.0 + lax.erf(ht * (2.0 ** -0.5)))).astype(BF16)

    irows = pl.ds(pl.multiple_of(eb * PEER_ROWS, PEER_ROWS), PEER_ROWS)
    shape = (BF16_ROWS, LANES)

    def per_lane_block(lb, carry):
        lanes = pl.ds(pl.multiple_of(lb * LANES, LANES), LANES)
        tau = [jnp.broadcast_to(tau_ref[0, h:h + 1, lanes], shape) for h in range(PEER_HEADS)]
        for il in range(PEER_ROWS):
            s1 = [jnp.broadcast_to(s_ref[2 * h, irows, lanes][il:il + 1], shape) for h in range(PEER_HEADS)]
            e1 = [jnp.broadcast_to(e_ref[2 * h, irows, lanes][il:il + 1] * tau_ref[1, h:h + 1, lanes], shape)
                  for h in range(PEER_HEADS)]

            def per_jb(jb, carry2):
                rows = pl.ds(pl.multiple_of(jb * BF16_ROWS, BF16_ROWS), BF16_ROWS)
                w = jnp.zeros(shape, F32)
                for h in range(PEER_HEADS):
                    t = s1[h] + s_ref[2 * h + 1, rows, lanes]
                    w = w + jnp.where(t >= tau[h], e1[h] * e_ref[2 * h + 1, rows, lanes], 0.0)
                arow = pl.ds(pl.multiple_of(il * PEER_NKEYS + jb * BF16_ROWS, BF16_ROWS), BF16_ROWS)
                act_ref[arow, lanes] = (act_ref[arow, lanes].astype(F32) * w).astype(BF16)
                return carry2

            lax.fori_loop(0, PEER_NKEYS // BF16_ROWS, per_jb, 0)
        return carry

    lax.fori_loop(0, nlb, per_lane_block, 0)
    acc_ref[...] += jnp.dot(vt_ref[...], act_ref[...], preferred_element_type=F32)

    @pl.when(eb == pl.num_programs(1) - 1)
    def _finish():
        o_ref[...] = h_ref[...] + acc_ref[...].T


def _peer(h, norm_g, wqt, keys, u, vt):
    t_total = h.shape[0]
    tt = min(PEER_TT, t_total)
    assert t_total % tt == 0 and tt % LANES == 0
    n_eb = PEER_EXPERTS // PEER_EB
    kern = functools.partial(_peer_kernel, tt=tt)
    return pl.pallas_call(
        kern,
        grid=(t_total // tt, n_eb),
        in_specs=[
            pl.BlockSpec((tt, D_MODEL), lambda t, e: (t, 0)),
            pl.BlockSpec((1, D_MODEL), lambda t, e: (0, 0)),
            pl.BlockSpec((2, PEER_HEADS * PEER_DKEY, D_MODEL), lambda t, e: (0, 0, 0)),
            pl.BlockSpec((2, 2 * PEER_HEADS, PEER_NKEYS, PEER_DKEY // 2), lambda t, e: (0, 0, 0, 0)),
            pl.BlockSpec((PEER_EB, D_MODEL), lambda t, e: (e, 0)),
            pl.BlockSpec((D_MODEL, PEER_EB), lambda t, e: (0, e)),
        ],
        out_specs=pl.BlockSpec((tt, D_MODEL), lambda t, e: (t, 0)),
        out_shape=jax.ShapeDtypeStruct((t_total, D_MODEL), F32),
        scratch_shapes=[
            pltpu.VMEM((tt, D_MODEL), BF16),
            pltpu.VMEM((2 * PEER_HEADS, PEER_NKEYS, tt), F32),
            pltpu.VMEM((2 * PEER_HEADS, PEER_NKEYS, tt), F32),
            pltpu.VMEM((2, PEER_TOPK, PEER_HEADS, tt), F32),
            pltpu.VMEM((2, PEER_HEADS, tt), F32),
            pltpu.VMEM((D_MODEL, tt), F32),
            pltpu.VMEM((PEER_EB, tt), BF16),
        ],
        compiler_params=_cparams("parallel", "arbitrary"),
        name="peer",
    )(h, norm_g, wqt, keys, u, vt)


TOKEN_TILE = 256


def _inproj_kernel(x_ref, g_ref, wr_ref, ww_ref, pr_ref, pw_ref):
    xn = _rms(x_ref[...], g_ref[...]).astype(BF16)
    pr_ref[...] = jnp.dot(xn, wr_ref[...], preferred_element_type=F32)
    pw_ref[...] = jnp.dot(xn, ww_ref[...], preferred_element_type=F32)


def _inproj(x, norm_g, w_ret, w_rw):
    t_total = x.shape[0]
    tm = min(TOKEN_TILE, t_total)
    assert t_total % tm == 0
    const = lambda shape: pl.BlockSpec(shape, lambda t: (0,) * len(shape))
    return pl.pallas_call(
        _inproj_kernel,
        grid=(t_total // tm,),
        in_specs=[pl.BlockSpec((tm, D_MODEL), lambda t: (t, 0)), const((1, D_MODEL)),
                  const((D_MODEL, RET_PROJ)), const((D_MODEL, RWKV_PROJ))],
        out_specs=[pl.BlockSpec((tm, RET_PROJ), lambda t: (t, 0)),
                   pl.BlockSpec((tm, RWKV_PROJ), lambda t: (t, 0))],
        out_shape=[jax.ShapeDtypeStruct((t_total, RET_PROJ), F32),
                   jax.ShapeDtypeStruct((t_total, RWKV_PROJ), F32)],
        compiler_params=_cparams("parallel"),
        name="inproj",
    )(x, norm_g, w_ret, w_rw)


def _rope_partner(x):
    lane = lax.broadcasted_iota(jnp.int32, x.shape, 1)
    up = pltpu.roll(x, x.shape[1] - RET_DK // 2, 1)
    down = pltpu.roll(x, RET_DK // 2, 1)
    return jnp.where(lane % RET_DK < RET_DK // 2, up, down)


def _retention_kernel(p_ref, s0_ref, cos_ref, sin_ref, dmat_ref, xz_ref, o_ref, so_ref, s_ref,
                      *, nb, g_chunk):
    ci = pl.program_id(1)
    qk = RET_HEADS * RET_DK

    @pl.when(ci == 0)
    def _init():
        s_ref[...] = s0_ref[...]

    cos = cos_ref[...]
    sin = sin_ref[...]

    def per_seq(n, carry):
        x = p_ref[n]
        q = x[:, :qk]
        k = x[:, qk:2 * qk]
        q = q * cos + _rope_partner(q) * sin
        k = (k * cos + _rope_partner(k) * sin) * (RET_DK ** -0.5)
        for h in range(RET_HEADS):
            sl = slice(h * RET_DK, (h + 1) * RET_DK)
            qh, kh = q[:, sl], k[:, sl]
            vh = x[:, 2 * qk + h * RET_DV:2 * qk + (h + 1) * RET_DV]
            gh = x[:, 2 * qk + RET_W + h * RET_DV:2 * qk + RET_W + (h + 1) * RET_DV]
            s_h = s_ref[n, h]
            xi = xz_ref[h, :, 0:1]
            zeta = xz_ref[h, :, 1:2]
            inner = _dot_nt(qh, kh, HIGHEST) * dmat_ref[h]
            o = (jnp.dot(inner, vh, precision=HIGHEST, preferred_element_type=F32)
                 + jnp.dot(qh, s_h, precision=HIGHEST, preferred_element_type=F32) * xi)
            s_ref[n, h] = s_h * g_chunk[h] + _dot_tn(kh * zeta, vh, HIGHEST)
            oc = o - jnp.mean(o, axis=-1, keepdims=True)
            on = oc * lax.rsqrt(jnp.mean(oc * oc, axis=-1, keepdims=True) + GN_EPS)
            o_ref[n, :, sl] = on * (gh * jax.nn.sigmoid(gh))
        return carry

    lax.fori_loop(0, nb, per_seq, 0)

    @pl.when(ci == pl.num_programs(1) - 1)
    def _done():
        so_ref[...] = s_ref[...]


def _retention(proj_ret, s0, pos):
    b, l, _ = proj_ret.shape
    c = RET_CHUNK if l % RET_CHUNK == 0 else l
    nb = 1 if c == RET_CHUNK else 16
    assert b % nb == 0
    half = RET_DK // 2
    inv = ROPE_BASE ** (-jnp.arange(half, dtype=F32) / half)
    ang = pos.astype(F32)[:, None] * inv[None, :]
    cos = jnp.tile(jnp.cos(ang), (1, 2 * RET_HEADS))
    sin = jnp.tile(jnp.concatenate([-jnp.sin(ang), jnp.sin(ang)], axis=1), (1, RET_HEADS))
    lg = jnp.log(1.0 - 2.0 ** (-5.0 - jnp.arange(RET_HEADS, dtype=F32)))
    idx = jnp.arange(c, dtype=F32)
    diff = idx[:, None] - idx[None, :]
    dmat = jnp.where(diff[None] >= 0, jnp.exp(jnp.maximum(diff, 0.0)[None] * lg[:, None, None]), 0.0)
    xi = jnp.exp((idx + 1.0)[None, :] * lg[:, None])
    zeta = jnp.exp((c - 1.0 - idx)[None, :] * lg[:, None])
    xz = jnp.stack([xi, zeta], axis=-1)
    g_chunk = tuple(math.exp(c * math.log(1.0 - 2.0 ** (-5.0 - h))) for h in range(RET_HEADS))
    kern = functools.partial(_retention_kernel, nb=nb, g_chunk=g_chunk)
    return pl.pallas_call(
        kern,
        grid=(b // nb, l // c),
        in_specs=[
            pl.BlockSpec((nb, c, RET_PROJ), lambda i, j: (i, j, 0)),
            pl.BlockSpec((nb, RET_HEADS, RET_DK, RET_DV), lambda i, j: (i, 0, 0, 0)),
            pl.BlockSpec((c, RET_HEADS * RET_DK), lambda i, j: (j, 0)),
            pl.BlockSpec((c, RET_HEADS * RET_DK), lambda i, j: (j, 0)),
            pl.BlockSpec((RET_HEADS, c, c), lambda i, j: (0, 0, 0)),
            pl.BlockSpec((RET_HEADS, c, 2), lambda i, j: (0, 0, 0)),
        ],
        out_specs=[pl.BlockSpec((nb, c, RET_W), lambda i, j: (i, j, 0)),
                   pl.BlockSpec((nb, RET_HEADS, RET_DK, RET_DV), lambda i, j: (i, 0, 0, 0))],
        out_shape=[jax.ShapeDtypeStruct((b, l, RET_W), F32),
                   jax.ShapeDtypeStruct((b, RET_HEADS, RET_DK, RET_DV), F32)],
        scratch_shapes=[pltpu.VMEM((nb, RET_HEADS, RET_DK, RET_DV), F32)],
        compiler_params=_cparams("parallel", "arbitrary"),
        name="retention",
    )(proj_ret, s0, cos, sin, dmat, xz)


def _softplus(x):
    return jnp.maximum(x, 0.0) + jnp.log1p(jnp.exp(-jnp.abs(x)))


def _rwkv_prep_kernel(x_ref, sh_ref, mu_ref, w0_ref, w2_ref, a0_ref, a2_ref, g2_ref, kks_ref, ka_ref,
                      rk_ref, ones_ref, r_ref, w_ref, k_ref, kk_ref, cc_ref, v_ref, gate_ref, bonus_ref,
                      carry_ref, *, nb, tb):
    ti = pl.program_id(1)
    x = x_ref[...]
    first = jnp.where(ti == 0, sh_ref[...], carry_ref[...])
    prev = pltpu.roll(x.reshape(nb * tb, RWKV_PROJ), 1, 0).reshape(nb, tb, RWKV_PROJ)
    tpos = lax.broadcasted_iota(jnp.int32, x.shape, 1)
    prev = jnp.where(tpos == 0, first, prev)
    carry_ref[...] = x[:, tb - 1:tb, :]
    f = (x + (prev - x) * mu_ref[...]).reshape(nb * tb, RWKV_PROJ)
    o1, o2, o3 = RWKV_W, 2 * RWKV_W, 3 * RWKV_W
    r, k7, v7 = f[:, :o1], f[:, o1:o2], f[:, o2:o3]
    fw = f[:, o3:o3 + DECAY_LORA]
    fa = f[:, o3 + DECAY_LORA:o3 + DECAY_LORA + AAA_LORA]
    fg = f[:, o3 + DECAY_LORA + AAA_LORA:]
    dot32 = functools.partial(jnp.dot, precision=HIGHEST, preferred_element_type=F32)
    w = -_softplus(-(w0_ref[...] + dot32(jnp.tanh(fw), w2_ref[...]))) - 0.5
    a = jax.nn.sigmoid(a0_ref[...] + dot32(fa, a2_ref[...]))
    gate = dot32(jax.nn.sigmoid(fg), g2_ref[...])
    kk = k7 * kks_ref[...]
    kk = kk / jnp.maximum(jnp.sqrt(dot32(kk * kk, ones_ref[...])), 1e-12)
    k = k7 * (1.0 + (a - 1.0) * ka_ref[...])
    out3 = lambda y: y.reshape(nb, tb, RWKV_W)
    r_ref[...] = out3(r)
    w_ref[...] = out3(jnp.exp(-jnp.exp(w)))
    k_ref[...] = out3(k)
    kk_ref[...] = out3(kk)
    cc_ref[...] = out3(kk * a)
    v_ref[...] = out3(v7)
    gate_ref[...] = out3(gate)
    bonus_ref[...] = out3(dot32(r * k * rk_ref[...], ones_ref[...]) * v7)


def _rwkv_prep(proj_rw, shift, wt):
    b, l, _ = proj_rw.shape
    tb = min(TOKEN_TILE, l)
    nb = max(1, 128 // tb)
    assert l % tb == 0 and b % nb == 0 and tb % SUBLANES == 0
    const = lambda shape: pl.BlockSpec(shape, lambda i, j: (0,) * len(shape))
    blk = pl.BlockSpec((nb, tb, RWKV_W), lambda i, j: (i, j, 0))
    kern = functools.partial(_rwkv_prep_kernel, nb=nb, tb=tb)
    return pl.pallas_call(
        kern,
        grid=(b // nb, l // tb),
        in_specs=[
            pl.BlockSpec((nb, tb, RWKV_PROJ), lambda i, j: (i, j, 0)),
            pl.BlockSpec((nb, 1, RWKV_PROJ), lambda i, j: (i, 0, 0)),
            const((1, RWKV_PROJ)), const((1, RWKV_W)), const((DECAY_LORA, RWKV_W)),
            const((1, RWKV_W)), const((AAA_LORA, RWKV_W)), const((GATE_LORA, RWKV_W)),
            const((1, RWKV_W)), const((1, RWKV_W)), const((1, RWKV_W)), const((RWKV_W, RWKV_W)),
        ],
        out_specs=[blk] * 8,
        out_shape=[jax.ShapeDtypeStruct((b, l, RWKV_W), F32)] * 8,
        scratch_shapes=[pltpu.VMEM((nb, 1, RWKV_PROJ), F32)],
        compiler_params=_cparams("parallel", "arbitrary"),
        name="rwkv_prep",
    )(proj_rw, shift.reshape(b, 1, RWKV_PROJ), wt['mu'], wt['w0'], wt['w2'], wt['a0'], wt['a2'],
      wt['g2'], wt['kk'], wt['ka'], wt['rk'], wt['head_ones'])


def _wkv_kernel(kk_ref, w_ref, cc_ref, k_ref, r_ref, v_ref, s0_ref, y_ref, so_ref, s_ref, *, tb, ni):
    ti = pl.program_id(1)

    @pl.when(ti == 0)
    def _init():
        s_ref[...] = s0_ref[0]

    def step(t, carry):
        kk, w, cc, k, r = kk_ref[0, t], w_ref[0, t], cc_ref[0, t], k_ref[0, t], r_ref[0, t]
        for ii in range(ni):
            s = s_ref[ii]
            sa = -jnp.sum(s * kk, axis=0, keepdims=True)
            sn = s * w + sa * cc + v_ref[0, t, ii:ii + 1, :] * k
            s_ref[ii] = sn
            y_ref[0, t, ii:ii + 1, :] = jnp.sum(sn * r, axis=0, keepdims=True)
        return carry

    lax.fori_loop(0, tb, step, 0)

    @pl.when(ti == pl.num_programs(1) - 1)
    def _done():
        so_ref[0] = s_ref[...]


def _wkv_scan(kk, w, cc, k, r, v, s0):
    g, l, ni, _ = v.shape
    tb = min(32, l)
    assert l % tb == 0
    jblk = pl.BlockSpec((1, tb, RWKV_N, LANES), lambda i, j: (i, j, 0, 0))
    iblk = pl.BlockSpec((1, tb, ni, LANES), lambda i, j: (i, j, 0, 0))
    sblk = pl.BlockSpec((1, ni, RWKV_N, LANES), lambda i, j: (i, 0, 0, 0))
    kern = functools.partial(_wkv_kernel, tb=tb, ni=ni)
    return pl.pallas_call(
        kern,
        grid=(g, l // tb),
        in_specs=[jblk] * 5 + [iblk, sblk],
        out_specs=[iblk, sblk],
        out_shape=[jax.ShapeDtypeStruct((g, l, ni, LANES), F32),
                   jax.ShapeDtypeStruct((g, ni, RWKV_N, LANES), F32)],
        scratch_shapes=[pltpu.VMEM((ni, RWKV_N, LANES), F32)],
        compiler_params=_cparams("parallel", "arbitrary"),
        name="wkv_scan",
    )(kk, w, cc, k, r, v, s0)


def _outproj_kernel(x_ref, oret_ref, y_ref, bonus_ref, gate_ref, lng_ref, lnb_ref, ones_ref,
                    wtop_ref, wbot_ref, o_ref):
    dot32 = functools.partial(jnp.dot, precision=HIGHEST, preferred_element_type=F32)
    y = y_ref[...]
    yc = y - dot32(y, ones_ref[...]) * (1.0 / RWKV_N)
    var = dot32(yc * yc, ones_ref[...]) * (1.0 / RWKV_N)
    o = yc * lax.rsqrt(var + RWKV_LN_EPS) * lng_ref[...] + lnb_ref[...]
    o = (o + bonus_ref[...]) * gate_ref[...]
    o_ref[...] = (x_ref[...]
                  + jnp.dot(oret_ref[...].astype(BF16), wtop_ref[...], preferred_element_type=F32)
                  + jnp.dot(o.astype(BF16), wbot_ref[...], preferred_element_type=F32))


def _outproj(x, o_ret, y, bonus, gate, wt):
    t_total = x.shape[0]
    tm = min(TOKEN_TILE, t_total)
    const = lambda shape: pl.BlockSpec(shape, lambda t: (0,) * len(shape))
    half = pl.BlockSpec((tm, RWKV_W), lambda t: (t, 0))
    full = pl.BlockSpec((tm, D_MODEL), lambda t: (t, 0))
    return pl.pallas_call(
        _outproj_kernel,
        grid=(t_total // tm,),
        in_specs=[full, half, half, half, half, const((1, RWKV_W)), const((1, RWKV_W)),
                  const((RWKV_W, RWKV_W)), const((RET_W, D_MODEL)), const((RWKV_W, D_MODEL))],
        out_specs=full,
        out_shape=jax.ShapeDtypeStruct((t_total, D_MODEL), F32),
        compiler_params=_cparams("parallel"),
        name="outproj",
    )(x, o_ret, y, bonus, gate, wt['ln_g'], wt['ln_b'], wt['head_ones'], wt['w_out_top'], wt['w_out_bot'])


def _ple_kernel(h_ref, p_ref, gn_ref, wg_ref, wp_ref, fn_ref, o_ref):
    h = h_ref[...]
    gate = jax.nn.sigmoid(jnp.dot(_rms(h, gn_ref[...]).astype(BF16), wg_ref[...], preferred_element_type=F32))
    h = h + gate * jnp.dot(p_ref[...].astype(BF16), wp_ref[...], preferred_element_type=F32)
    o_ref[...] = _rms(h, fn_ref[...])


def _ple(h, p, wt):
    t_total = h.shape[0]
    tm = min(TOKEN_TILE, t_total)
    const = lambda shape: pl.BlockSpec(shape, lambda t: (0,) * len(shape))
    full = pl.BlockSpec((tm, D_MODEL), lambda t: (t, 0))
    return pl.pallas_call(
        _ple_kernel,
        grid=(t_total // tm,),
        in_specs=[full, pl.BlockSpec((tm, PLE_DIM), lambda t: (t, 0)), const((1, D_MODEL)),
                  const((D_MODEL, D_MODEL)), const((PLE_DIM, D_MODEL)), const((1, D_MODEL))],
        out_specs=full,
        out_shape=jax.ShapeDtypeStruct((t_total, D_MODEL), F32),
        compiler_params=_cparams("parallel"),
        name="ple",
    )(h, p, wt['ple_norm'], wt['ple_gate'], wt['ple_proj'], wt['norm_final'])


def _prep_weights(p):
    row = lambda a: a.reshape(1, -1)
    lane_head = jnp.arange(RWKV_W) // RWKV_N
    return {
        'norm_mix': row(p['norm_mix'][0]),
        'w_ret': p['w_in'][0][:, :RET_PROJ].astype(BF16),
        'w_rw': p['w_in'][0][:, RET_PROJ:].astype(BF16),
        'mu': row(p['rw_mu'][0]), 'w0': row(p['rw_w0'][0]), 'w2': p['rw_w2'][0],
        'a0': row(p['rw_a0'][0]), 'a2': p['rw_a2'][0], 'g2': p['rw_g2'][0],
        'kk': row(p['rw_kk'][0]), 'ka': row(p['rw_ka'][0]), 'rk': row(p['rw_rk'][0]),
        'ln_g': row(p['rw_ln_g'][0]), 'ln_b': row(p['rw_ln_b'][0]),
        'head_ones': (lane_head[:, None] == lane_head[None, :]).astype(F32),
        'w_out_top': p['w_out'][0][:RET_W].astype(BF16),
        'w_out_bot': p['w_out'][0][RET_W:].astype(BF16),
        'norm_ffn': row(p['norm_ffn'][0]),
        'wqt': jnp.stack(_split(p['peer_wq'][0].T)),
        'keys': jnp.stack(_split(p['peer_keys'][0].reshape(2 * PEER_HEADS, PEER_NKEYS, PEER_DKEY // 2))),
        'u': p['peer_u'][0].astype(BF16),
        'vt': p['peer_v'][0].astype(BF16).T,
        'ple_norm': row(p['ple_norm'][0]),
        'ple_gate': p['ple_gate'][0].astype(BF16),
        'ple_proj': p['ple_proj'][0].astype(BF16),
        'norm_final': row(p['norm_final']),
    }


def _chains_to_lanes(a, split_values):
    b, l, _ = a.shape
    a = a.reshape(b, l, RWKV_HEADS, RWKV_N)
    if b * RWKV_HEADS * 2 == LANES:
        if split_values:
            a = a.reshape(b, l, RWKV_HEADS, 2, RWKV_N // 2).transpose(1, 4, 3, 0, 2)
            return a.reshape(1, l, RWKV_N // 2, LANES)
        a = a.transpose(1, 3, 0, 2).reshape(l, RWKV_N, LANES // 2)
        return jnp.concatenate([a, a], axis=-1)[None]
    assert b == LANES
    return a.transpose(2, 1, 3, 0)


def _values_from_lanes(y, b):
    g, l, ni, _ = y.shape
    if g == 1:
        y = y.reshape(l, ni, 2, b, RWKV_HEADS).transpose(3, 0, 4, 2, 1)
    else:
        y = y.transpose(3, 1, 0, 2)
    return y.reshape(b, l, RWKV_W)


def _state_to_lanes(s):
    b = s.shape[0]
    if b * RWKV_HEADS * 2 == LANES:
        s = s.reshape(b, RWKV_HEADS, 2, RWKV_N // 2, RWKV_N).transpose(3, 4, 2, 0, 1)
        return s.reshape(1, RWKV_N // 2, RWKV_N, LANES)
    return s.transpose(1, 2, 3, 0)


def _state_from_lanes(s, b):
    g, ni, _, _ = s.shape
    if g == 1:
        s = s.reshape(ni, RWKV_N, 2, b, RWKV_HEADS).transpose(3, 4, 2, 0, 1)
        return s.reshape(b, RWKV_HEADS, RWKV_N, RWKV_N)
    return s.transpose(3, 0, 1, 2)


def _layer(x, p_l, pos, s_ret, s_wkv, s_shift, wt):
    b, l, _ = x.shape
    x2 = x.reshape(b * l, D_MODEL)
    proj_ret, proj_rw = _inproj(x2, wt['norm_mix'], wt['w_ret'], wt['w_rw'])
    o_ret, s_ret_new = _retention(proj_ret.reshape(b, l, RET_PROJ), s_ret, pos)
    proj_rw = proj_rw.reshape(b, l, RWKV_PROJ)
    new_shift = proj_rw[:, -1]
    r, w, k, kk, cc, v, gate, bonus = _rwkv_prep(proj_rw, s_shift, wt)
    y, s_lanes = _wkv_scan(*(_chains_to_lanes(a, False) for a in (kk, w, cc, k, r)),
                           _chains_to_lanes(v, True), _state_to_lanes(s_wkv))
    y = _values_from_lanes(y, b)
    s_wkv_new = _state_from_lanes(s_lanes, b)
    flat = lambda a: a.reshape(b * l, -1)
    h = _outproj(x2, flat(o_ret), flat(y), flat(bonus), flat(gate), wt)
    h = _peer(h, wt['norm_ffn'], wt['wqt'], wt['keys'], wt['u'], wt['vt'])
    y_out = _ple(h, p_l.reshape(b * l, PLE_DIM), wt)
    return y_out.reshape(b, l, D_MODEL), s_ret_new, s_wkv_new, new_shift


def kernel(x_prompt, x_sample, state_ret, state_wkv, state_shift, p_prompt, p_sample, norm_mix, w_in, rw_mu, rw_w0, rw_w2, rw_a0, rw_a2, rw_g2, rw_kk, rw_ka, rw_rk, rw_ln_g, rw_ln_b, w_out, norm_ffn, peer_wq, peer_keys, peer_u, peer_v, ple_norm, ple_gate, ple_proj, norm_final):
    assert norm_mix.shape[0] == 1, "single-layer trunk"
    wt = _prep_weights(dict(
        norm_mix=norm_mix, w_in=w_in, rw_mu=rw_mu, rw_w0=rw_w0, rw_w2=rw_w2, rw_a0=rw_a0, rw_a2=rw_a2,
        rw_g2=rw_g2, rw_kk=rw_kk, rw_ka=rw_ka, rw_rk=rw_rk, rw_ln_g=rw_ln_g, rw_ln_b=rw_ln_b, w_out=w_out,
        norm_ffn=norm_ffn, peer_wq=peer_wq, peer_keys=peer_keys, peer_u=peer_u, peer_v=peer_v,
        ple_norm=ple_norm, ple_gate=ple_gate, ple_proj=ple_proj, norm_final=norm_final))
    bp, lp, _ = x_prompt.shape
    bs, ls, _ = x_sample.shape
    pos_p = jnp.arange(lp, dtype=jnp.int32)
    pos_s = PAST_LEN + jnp.arange(ls, dtype=jnp.int32)
    zero_ret = jnp.zeros((bp, RET_HEADS, RET_DK, RET_DV), F32)
    zero_wkv = jnp.zeros((bp, RWKV_HEADS, RWKV_N, RWKV_N), F32)
    zero_shift = jnp.zeros((bp, RWKV_PROJ), F32)
    yp, rp, wp, sp = _layer(x_prompt, p_prompt[0], pos_p, zero_ret, zero_wkv, zero_shift, wt)
    ys, rs, ws, ss = _layer(x_sample, p_sample[0], pos_s, state_ret[0], state_wkv[0], state_shift[0], wt)
    return (yp, ys, rp[None], wp[None], sp[None], rs[None], ws[None], ss[None])
```

```python
import functools
import math

import jax
import jax.numpy as jnp
from jax import lax
from jax.experimental import pallas as pl
from jax.experimental.pallas import tpu as pltpu

F32 = jnp.float32
BF16 = jnp.bfloat16
HIGHEST = lax.Precision.HIGHEST

D_MODEL = 1024
PAST_LEN = 16384
RET_HEADS = 8
RET_DK = 64
RET_DV = 64
RET_W = RET_HEADS * RET_DV
RWKV_HEADS = 8
RWKV_N = 64
RWKV_W = RWKV_HEADS * RWKV_N
DECAY_LORA = 64
AAA_LORA = 64
GATE_LORA = 128
RET_PROJ = 2 * RET_HEADS * RET_DK + 2 * RET_W
RWKV_PROJ = 3 * RWKV_W + DECAY_LORA + AAA_LORA + GATE_LORA
RET_CHUNK = 128
ROPE_BASE = 10000.0
PEER_HEADS = 8
PEER_NKEYS = 128
PEER_EXPERTS = PEER_NKEYS * PEER_NKEYS
PEER_DKEY = 256
PEER_TOPK = 16
PLE_DIM = 256
RMS_EPS = 1e-6
GN_EPS = 1e-5
RWKV_LN_EPS = 64e-5

LANES = 128
SUBLANES = 8
BF16_ROWS = 16
VMEM_LIMIT = 56 * 1024 * 1024

NEG_INF = float("-inf")


def _cparams(*sem):
    return pltpu.CompilerParams(dimension_semantics=sem, vmem_limit_bytes=VMEM_LIMIT)


def _rms(x, g):
    return x * lax.rsqrt(jnp.mean(x * x, axis=-1, keepdims=True) + RMS_EPS) * g


def _split(x):
    hi = x.astype(BF16)
    return hi, (x - hi.astype(F32)).astype(BF16)


def _dot_nt(a, b, precision=None):
    return lax.dot_general(a, b, (((1,), (1,)), ((), ())), precision=precision,
                           preferred_element_type=F32)


def _dot_tn(a, b, precision=None):
    return lax.dot_general(a, b, (((0,), (0,)), ((), ())), precision=precision,
                           preferred_element_type=F32)


PEER_TT = 512
PEER_ROWS = 8
PEER_EB = PEER_ROWS * PEER_NKEYS
PEER_CHUNK = 256
PEER_NTOP = PEER_TOPK + 1
_CANDS = tuple((a, b) for a in range(PEER_NTOP) for b in range(PEER_NTOP)
               if (a + 1) * (b + 1) <= PEER_NTOP)


def _peer_kernel(h_ref, g_ref, wqt_ref, keys_ref, u_ref, vt_ref, o_ref,
                 xn_ref, s_ref, e1_ref, se2_ref, top_ref, tau_ref, acc_ref, ht_ref, act_ref, *, tt):
    eb = pl.program_id(1)
    nlb = tt // LANES

    @pl.when(eb == 0)
    def _prepare():
        xn = _rms(h_ref[...], g_ref[...])
        xh, xl = _split(xn)
        xn_ref[...] = xh
        qt = (_dot_nt(wqt_ref[0], xh) + _dot_nt(wqt_ref[0], xl) + _dot_nt(wqt_ref[1], xh))
        for hc in range(2 * PEER_HEADS):
            qh, ql = _split(qt[hc * PEER_NKEYS:(hc + 1) * PEER_NKEYS, :])
            s_ref[hc] = (jnp.dot(keys_ref[0, hc], qh, preferred_element_type=F32)
                         + jnp.dot(keys_ref[0, hc], ql, preferred_element_type=F32)
                         + jnp.dot(keys_ref[1, hc], qh, preferred_element_type=F32))

        def per_lane_block(lb, carry):
            lanes = pl.ds(pl.multiple_of(lb * LANES, LANES), LANES)
            for hc in range(2 * PEER_HEADS):
                h, c = divmod(hc, 2)
                work = s_ref[hc, :, lanes]
                ex = jnp.exp(work - jnp.max(work, axis=0, keepdims=True))
                if c == 0:
                    e1_ref[h, :, lanes] = ex
                else:
                    for rb in range(PEER_NKEYS // SUBLANES):
                        rows = slice(rb * SUBLANES, (rb + 1) * SUBLANES)
                        se2_ref[lb, rb, 2 * h] = work[rows]
                        se2_ref[lb, rb, 2 * h + 1] = ex[rows]
                for r in range(PEER_NTOP):
                    m = jnp.max(work, axis=0, keepdims=True)
                    top_ref[c, r, pl.ds(h, 1), lanes] = m
                    work = jnp.where(work == m, NEG_INF, work)
            v1 = [top_ref[0, r, :, lanes] for r in range(PEER_NTOP)]
            v2 = [top_ref[1, r, :, lanes] for r in range(PEER_NTOP)]
            cands = [v1[a] + v2[b] for (a, b) in _CANDS]
            cv = []
            for r in range(PEER_TOPK + 1):
                m = functools.reduce(jnp.maximum, cands)
                cv.append(m)
                if r < PEER_TOPK:
                    cands = [jnp.where(x == m, NEG_INF, x) for x in cands]
            z = functools.reduce(lambda a, b: a + b, [jnp.exp(x - cv[0]) for x in cv[:PEER_TOPK]])
            tau_ref[0, :, lanes] = 0.5 * (cv[PEER_TOPK - 1] + cv[PEER_TOPK])
            tau_ref[1, :, lanes] = 1.0 / z
            return carry

        lax.fori_loop(0, nlb, per_lane_block, 0)
        acc_ref[...] = jnp.zeros_like(acc_ref)

    irows = pl.ds(pl.multiple_of(eb * PEER_ROWS, PEER_ROWS), PEER_ROWS)
    shape = (SUBLANES, LANES)

    def gate_rows(il, lb):
        lanes = slice(lb * LANES, (lb + 1) * LANES)
        tau = tau_ref[0, :, lanes]
        rz = tau_ref[1, :, lanes]
        theta, e1 = [], []
        for h in range(PEER_HEADS):
            theta.append(jnp.broadcast_to(tau[h:h + 1] - s_ref[2 * h, irows, lanes][il:il + 1], shape))
            e1.append(jnp.broadcast_to(e1_ref[h, irows, lanes][il:il + 1] * rz[h:h + 1], shape))
        for jb in range(PEER_NKEYS // BF16_ROWS):
            pieces = []
            for half in range(2):
                rb = 2 * jb + half
                j0 = rb * SUBLANES
                terms = [jnp.where(se2_ref[lb, rb, 2 * h] >= theta[h],
                                   e1[h] * se2_ref[lb, rb, 2 * h + 1], 0.0) for h in range(PEER_HEADS)]
                while len(terms) > 1:
                    terms = [terms[i] + terms[i + 1] for i in range(0, len(terms), 2)]
                r0 = il * PEER_NKEYS + j0
                hv = ht_ref[r0:r0 + SUBLANES, lanes]
                pieces.append(0.5 * hv * (1.0 + lax.erf(hv * (2.0 ** -0.5))) * terms[0])
            a0 = il * PEER_NKEYS + jb * BF16_ROWS
            act_ref[a0:a0 + BF16_ROWS, lanes] = jnp.concatenate(pieces, axis=0).astype(BF16)

    n_ck = PEER_EB // PEER_CHUNK
    chunk = lambda ck: slice(ck * PEER_CHUNK, (ck + 1) * PEER_CHUNK)

    def hval(ck):
        ht_ref[chunk(ck), :] = _dot_nt(u_ref[chunk(ck), :], xn_ref[...])

    hval(0)
    for ck in range(n_ck):
        if ck + 1 < n_ck:
            hval(ck + 1)
        for il in range(ck * PEER_CHUNK // PEER_NKEYS, (ck + 1) * PEER_CHUNK // PEER_NKEYS):
            for lb in range(nlb):
                gate_rows(il, lb)
        acc_ref[...] += jnp.dot(vt_ref[:, chunk(ck)], act_ref[chunk(ck), :], preferred_element_type=F32)

    @pl.when(eb == pl.num_programs(1) - 1)
    def _finish():
        o_ref[...] = h_ref[...] + acc_ref[...].T


def _peer(h, norm_g, wqt, keys, u, vt):
    t_total = h.shape[0]
    tt = min(PEER_TT, t_total)
    assert t_total % tt == 0 and tt % LANES == 0
    n_eb = PEER_EXPERTS // PEER_EB
    kern = functools.partial(_peer_kernel, tt=tt)
    return pl.pallas_call(
        kern,
        grid=(t_total // tt, n_eb),
        in_specs=[
            pl.BlockSpec((tt, D_MODEL), lambda t, e: (t, 0)),
            pl.BlockSpec((1, D_MODEL), lambda t, e: (0, 0)),
            pl.BlockSpec((2, PEER_HEADS * PEER_DKEY, D_MODEL), lambda t, e: (0, 0, 0)),
            pl.BlockSpec((2, 2 * PEER_HEADS, PEER_NKEYS, PEER_DKEY // 2), lambda t, e: (0, 0, 0, 0)),
            pl.BlockSpec((PEER_EB, D_MODEL), lambda t, e: (e, 0)),
            pl.BlockSpec((D_MODEL, PEER_EB), lambda t, e: (0, e)),
        ],
        out_specs=pl.BlockSpec((tt, D_MODEL), lambda t, e: (t, 0)),
        out_shape=jax.ShapeDtypeStruct((t_total, D_MODEL), F32),
        scratch_shapes=[
            pltpu.VMEM((tt, D_MODEL), BF16),
            pltpu.VMEM((2 * PEER_HEADS, PEER_NKEYS, tt), F32),
            pltpu.VMEM((PEER_HEADS, PEER_NKEYS, tt), F32),
            pltpu.VMEM((tt // LANES, PEER_NKEYS // SUBLANES, 2 * PEER_HEADS, SUBLANES, LANES), F32),
            pltpu.VMEM((2, PEER_NTOP, PEER_HEADS, tt), F32),
            pltpu.VMEM((2, PEER_HEADS, tt), F32),
            pltpu.VMEM((D_MODEL, tt), F32),
            pltpu.VMEM((PEER_EB, tt), F32),
            pltpu.VMEM((PEER_EB, tt), BF16),
        ],
        compiler_params=_cparams("parallel", "arbitrary"),
        name="peer",
    )(h, norm_g, wqt, keys, u, vt)


TOKEN_TILE = 256


def _inproj_kernel(x_ref, g_ref, wr_ref, ww_ref, pr_ref, pw_ref):
    xn = _rms(x_ref[...], g_ref[...]).astype(BF16)
    pr_ref[...] = jnp.dot(xn, wr_ref[...], preferred_element_type=F32)
    pw_ref[...] = jnp.dot(xn, ww_ref[...], preferred_element_type=F32)


def _inproj(x, norm_g, w_ret, w_rw):
    t_total = x.shape[0]
    tm = min(TOKEN_TILE, t_total)
    assert t_total % tm == 0
    const = lambda shape: pl.BlockSpec(shape, lambda t: (0,) * len(shape))
    return pl.pallas_call(
        _inproj_kernel,
        grid=(t_total // tm,),
        in_specs=[pl.BlockSpec((tm, D_MODEL), lambda t: (t, 0)), const((1, D_MODEL)),
                  const((D_MODEL, RET_PROJ)), const((D_MODEL, RWKV_PROJ))],
        out_specs=[pl.BlockSpec((tm, RET_PROJ), lambda t: (t, 0)),
                   pl.BlockSpec((tm, RWKV_PROJ), lambda t: (t, 0))],
        out_shape=[jax.ShapeDtypeStruct((t_total, RET_PROJ), F32),
                   jax.ShapeDtypeStruct((t_total, RWKV_PROJ), F32)],
        compiler_params=_cparams("parallel"),
        name="inproj",
    )(x, norm_g, w_ret, w_rw)


def _rope_partner(x):
    lane = lax.broadcasted_iota(jnp.int32, x.shape, 1)
    up = pltpu.roll(x, x.shape[1] - RET_DK // 2, 1)
    down = pltpu.roll(x, RET_DK // 2, 1)
    return jnp.where(lane % RET_DK < RET_DK // 2, up, down)


def _retention_kernel(p_ref, s0_ref, cos_ref, sin_ref, dmat_ref, xz_ref, o_ref, so_ref, s_ref,
                      *, nb, g_chunk):
    ci = pl.program_id(1)
    qk = RET_HEADS * RET_DK

    @pl.when(ci == 0)
    def _init():
        s_ref[...] = s0_ref[...]

    cos = cos_ref[...]
    sin = sin_ref[...]

    def per_seq(n, carry):
        x = p_ref[n]
        q = x[:, :qk]
        k = x[:, qk:2 * qk]
        q = q * cos + _rope_partner(q) * sin
        k = (k * cos + _rope_partner(k) * sin) * (RET_DK ** -0.5)
        for h in range(RET_HEADS):
            sl = slice(h * RET_DK, (h + 1) * RET_DK)
            qh, kh = q[:, sl], k[:, sl]
            vh = x[:, 2 * qk + h * RET_DV:2 * qk + (h + 1) * RET_DV]
            gh = x[:, 2 * qk + RET_W + h * RET_DV:2 * qk + RET_W + (h + 1) * RET_DV]
            s_h = s_ref[n, h]
            xi = xz_ref[h, :, 0:1]
            zeta = xz_ref[h, :, 1:2]
            inner = _dot_nt(qh, kh, HIGHEST) * dmat_ref[h]
            o = (jnp.dot(inner, vh, precision=HIGHEST, preferred_element_type=F32)
                 + jnp.dot(qh, s_h, precision=HIGHEST, preferred_element_type=F32) * xi)
            s_ref[n, h] = s_h * g_chunk[h] + _dot_tn(kh * zeta, vh, HIGHEST)
            oc = o - jnp.mean(o, axis=-1, keepdims=True)
            on = oc * lax.rsqrt(jnp.mean(oc * oc, axis=-1, keepdims=True) + GN_EPS)
            o_ref[n, :, sl] = on * (gh * jax.nn.sigmoid(gh))
        return carry

    lax.fori_loop(0, nb, per_seq, 0)

    @pl.when(ci == pl.num_programs(1) - 1)
    def _done():
        so_ref[...] = s_ref[...]


def _retention(proj_ret, s0, pos):
    b, l, _ = proj_ret.shape
    c = RET_CHUNK if l % RET_CHUNK == 0 else l
    nb = 1 if c == RET_CHUNK else 16
    assert b % nb == 0
    half = RET_DK // 2
    inv = ROPE_BASE ** (-jnp.arange(half, dtype=F32) / half)
    ang = pos.astype(F32)[:, None] * inv[None, :]
    cos = jnp.tile(jnp.cos(ang), (1, 2 * RET_HEADS))
    sin = jnp.tile(jnp.concatenate([-jnp.sin(ang), jnp.sin(ang)], axis=1), (1, RET_HEADS))
    lg = jnp.log(1.0 - 2.0 ** (-5.0 - jnp.arange(RET_HEADS, dtype=F32)))
    idx = jnp.arange(c, dtype=F32)
    diff = idx[:, None] - idx[None, :]
    dmat = jnp.where(diff[None] >= 0, jnp.exp(jnp.maximum(diff, 0.0)[None] * lg[:, None, None]), 0.0)
    xi = jnp.exp((idx + 1.0)[None, :] * lg[:, None])
    zeta = jnp.exp((c - 1.0 - idx)[None, :] * lg[:, None])
    xz = jnp.stack([xi, zeta], axis=-1)
    g_chunk = tuple(math.exp(c * math.log(1.0 - 2.0 ** (-5.0 - h))) for h in range(RET_HEADS))
    kern = functools.partial(_retention_kernel, nb=nb, g_chunk=g_chunk)
    return pl.pallas_call(
        kern,
        grid=(b // nb, l // c),
        in_specs=[
            pl.BlockSpec((nb, c, RET_PROJ), lambda i, j: (i, j, 0)),
            pl.BlockSpec((nb, RET_HEADS, RET_DK, RET_DV), lambda i, j: (i, 0, 0, 0)),
            pl.BlockSpec((c, RET_HEADS * RET_DK), lambda i, j: (j, 0)),
            pl.BlockSpec((c, RET_HEADS * RET_DK), lambda i, j: (j, 0)),
            pl.BlockSpec((RET_HEADS, c, c), lambda i, j: (0, 0, 0)),
            pl.BlockSpec((RET_HEADS, c, 2), lambda i, j: (0, 0, 0)),
        ],
        out_specs=[pl.BlockSpec((nb, c, RET_W), lambda i, j: (i, j, 0)),
                   pl.BlockSpec((nb, RET_HEADS, RET_DK, RET_DV), lambda i, j: (i, 0, 0, 0))],
        out_shape=[jax.ShapeDtypeStruct((b, l, RET_W), F32),
                   jax.ShapeDtypeStruct((b, RET_HEADS, RET_DK, RET_DV), F32)],
        scratch_shapes=[pltpu.VMEM((nb, RET_HEADS, RET_DK, RET_DV), F32)],
        compiler_params=_cparams("parallel", "arbitrary"),
        name="retention",
    )(proj_ret, s0, cos, sin, dmat, xz)


def _softplus(x):
    return jnp.maximum(x, 0.0) + jnp.log1p(jnp.exp(-jnp.abs(x)))


def _rwkv_prep_kernel(x_ref, sh_ref, mu_ref, w0_ref, w2_ref, a0_ref, a2_ref, g2_ref, kks_ref, ka_ref,
                      rk_ref, ones_ref, r_ref, w_ref, k_ref, kk_ref, cc_ref, v_ref, gate_ref, bonus_ref,
                      carry_ref, *, nb, tb):
    ti = pl.program_id(1)
    x = x_ref[...]
    first = jnp.where(ti == 0, sh_ref[...], carry_ref[...])
    prev = pltpu.roll(x.reshape(nb * tb, RWKV_PROJ), 1, 0).reshape(nb, tb, RWKV_PROJ)
    tpos = lax.broadcasted_iota(jnp.int32, x.shape, 1)
    prev = jnp.where(tpos == 0, first, prev)
    carry_ref[...] = x[:, tb - 1:tb, :]
    f = (x + (prev - x) * mu_ref[...]).reshape(nb * tb, RWKV_PROJ)
    o1, o2, o3 = RWKV_W, 2 * RWKV_W, 3 * RWKV_W
    r, k7, v7 = f[:, :o1], f[:, o1:o2], f[:, o2:o3]
    fw = f[:, o3:o3 + DECAY_LORA]
    fa = f[:, o3 + DECAY_LORA:o3 + DECAY_LORA + AAA_LORA]
    fg = f[:, o3 + DECAY_LORA + AAA_LORA:]
    dot32 = functools.partial(jnp.dot, precision=HIGHEST, preferred_element_type=F32)
    w = -_softplus(-(w0_ref[...] + dot32(jnp.tanh(fw), w2_ref[...]))) - 0.5
    a = jax.nn.sigmoid(a0_ref[...] + dot32(fa, a2_ref[...]))
    gate = dot32(jax.nn.sigmoid(fg), g2_ref[...])
    kk = k7 * kks_ref[...]
    kk = kk / jnp.maximum(jnp.sqrt(dot32(kk * kk, ones_ref[...])), 1e-12)
    k = k7 * (1.0 + (a - 1.0) * ka_ref[...])
    out3 = lambda y: y.reshape(nb, tb, RWKV_W)
    r_ref[...] = out3(r)
    w_ref[...] = out3(jnp.exp(-jnp.exp(w)))
    k_ref[...] = out3(k)
    kk_ref[...] = out3(kk)
    cc_ref[...] = out3(kk * a)
    v_ref[...] = out3(v7)
    gate_ref[...] = out3(gate)
    bonus_ref[...] = out3(dot32(r * k * rk_ref[...], ones_ref[...]) * v7)


def _rwkv_prep(proj_rw, shift, wt):
    b, l, _ = proj_rw.shape
    tb = min(TOKEN_TILE, l)
    nb = max(1, 128 // tb)
    assert l % tb == 0 and b % nb == 0 and tb % SUBLANES == 0
    const = lambda shape: pl.BlockSpec(shape, lambda i, j: (0,) * len(shape))
    blk = pl.BlockSpec((nb, tb, RWKV_W), lambda i, j: (i, j, 0))
    kern = functools.partial(_rwkv_prep_kernel, nb=nb, tb=tb)
    return pl.pallas_call(
        kern,
        grid=(b // nb, l // tb),
        in_specs=[
            pl.BlockSpec((nb, tb, RWKV_PROJ), lambda i, j: (i, j, 0)),
            pl.BlockSpec((nb, 1, RWKV_PROJ), lambda i, j: (i, 0, 0)),
            const((1, RWKV_PROJ)), const((1, RWKV_W)), const((DECAY_LORA, RWKV_W)),
            const((1, RWKV_W)), const((AAA_LORA, RWKV_W)), const((GATE_LORA, RWKV_W)),
            const((1, RWKV_W)), const((1, RWKV_W)), const((1, RWKV_W)), const((RWKV_W, RWKV_W)),
        ],
        out_specs=[blk] * 8,
        out_shape=[jax.ShapeDtypeStruct((b, l, RWKV_W), F32)] * 8,
        scratch_shapes=[pltpu.VMEM((nb, 1, RWKV_PROJ), F32)],
        compiler_params=_cparams("parallel", "arbitrary"),
        name="rwkv_prep",
    )(proj_rw, shift.reshape(b, 1, RWKV_PROJ), wt['mu'], wt['w0'], wt['w2'], wt['a0'], wt['a2'],
      wt['g2'], wt['kk'], wt['ka'], wt['rk'], wt['head_ones'])


def _wkv_kernel(kk_ref, w_ref, cc_ref, k_ref, r_ref, v_ref, s0_ref, y_ref, so_ref, s_ref, *, tb, ni):
    ti = pl.program_id(1)

    @pl.when(ti == 0)
    def _init():
        s_ref[...] = s0_ref[0]

    def step(t, carry):
        kk, w, cc, k, r = kk_ref[0, t], w_ref[0, t], cc_ref[0, t], k_ref[0, t], r_ref[0, t]
        for ii in range(ni):
            s = s_ref[ii]
            sa = -jnp.sum(s * kk, axis=0, keepdims=True)
            sn = s * w + sa * cc + v_ref[0, t, ii:ii + 1, :] * k
            s_ref[ii] = sn
            y_ref[0, t, ii:ii + 1, :] = jnp.sum(sn * r, axis=0, keepdims=True)
        return carry

    lax.fori_loop(0, tb, step, 0)

    @pl.when(ti == pl.num_programs(1) - 1)
    def _done():
        so_ref[0] = s_ref[...]


def _wkv_scan(kk, w, cc, k, r, v, s0):
    g, l, ni, _ = v.shape
    tb = min(32, l)
    assert l % tb == 0
    jblk = pl.BlockSpec((1, tb, RWKV_N, LANES), lambda i, j: (i, j, 0, 0))
    iblk = pl.BlockSpec((1, tb, ni, LANES), lambda i, j: (i, j, 0, 0))
    sblk = pl.BlockSpec((1, ni, RWKV_N, LANES), lambda i, j: (i, 0, 0, 0))
    kern = functools.partial(_wkv_kernel, tb=tb, ni=ni)
    return pl.pallas_call(
        kern,
        grid=(g, l // tb),
        in_specs=[jblk] * 5 + [iblk, sblk],
        out_specs=[iblk, sblk],
        out_shape=[jax.ShapeDtypeStruct((g, l, ni, LANES), F32),
                   jax.ShapeDtypeStruct((g, ni, RWKV_N, LANES), F32)],
        scratch_shapes=[pltpu.VMEM((ni, RWKV_N, LANES), F32)],
        compiler_params=_cparams("parallel", "arbitrary"),
        name="wkv_scan",
    )(kk, w, cc, k, r, v, s0)


def _outproj_kernel(x_ref, oret_ref, y_ref, bonus_ref, gate_ref, lng_ref, lnb_ref, ones_ref,
                    wtop_ref, wbot_ref, o_ref):
    dot32 = functools.partial(jnp.dot, precision=HIGHEST, preferred_element_type=F32)
    y = y_ref[...]
    yc = y - dot32(y, ones_ref[...]) * (1.0 / RWKV_N)
    var = dot32(yc * yc, ones_ref[...]) * (1.0 / RWKV_N)
    o = yc * lax.rsqrt(var + RWKV_LN_EPS) * lng_ref[...] + lnb_ref[...]
    o = (o + bonus_ref[...]) * gate_ref[...]
    o_ref[...] = (x_ref[...]
                  + jnp.dot(oret_ref[...].astype(BF16), wtop_ref[...], preferred_element_type=F32)
                  + jnp.dot(o.astype(BF16), wbot_ref[...], preferred_element_type=F32))


def _outproj(x, o_ret, y, bonus, gate, wt):
    t_total = x.shape[0]
    tm = min(TOKEN_TILE, t_total)
    const = lambda shape: pl.BlockSpec(shape, lambda t: (0,) * len(shape))
    half = pl.BlockSpec((tm, RWKV_W), lambda t: (t, 0))
    full = pl.BlockSpec((tm, D_MODEL), lambda t: (t, 0))
    return pl.pallas_call(
        _outproj_kernel,
        grid=(t_total // tm,),
        in_specs=[full, half, half, half, half, const((1, RWKV_W)), const((1, RWKV_W)),
                  const((RWKV_W, RWKV_W)), const((RET_W, D_MODEL)), const((RWKV_W, D_MODEL))],
        out_specs=full,
        out_shape=jax.ShapeDtypeStruct((t_total, D_MODEL), F32),
        compiler_params=_cparams("parallel"),
        name="outproj",
    )(x, o_ret, y, bonus, gate, wt['ln_g'], wt['ln_b'], wt['head_ones'], wt['w_out_top'], wt['w_out_bot'])


def _ple_kernel(h_ref, p_ref, gn_ref, wg_ref, wp_ref, fn_ref, o_ref):
    h = h_ref[...]
    gate = jax.nn.sigmoid(jnp.dot(_rms(h, gn_ref[...]).astype(BF16), wg_ref[...], preferred_element_type=F32))
    h = h + gate * jnp.dot(p_ref[...].astype(BF16), wp_ref[...], preferred_element_type=F32)
    o_ref[...] = _rms(h, fn_ref[...])


def _ple(h, p, wt):
    t_total = h.shape[0]
    tm = min(TOKEN_TILE, t_total)
    const = lambda shape: pl.BlockSpec(shape, lambda t: (0,) * len(shape))
    full = pl.BlockSpec((tm, D_MODEL), lambda t: (t, 0))
    return pl.pallas_call(
        _ple_kernel,
        grid=(t_total // tm,),
        in_specs=[full, pl.BlockSpec((tm, PLE_DIM), lambda t: (t, 0)), const((1, D_MODEL)),
                  const((D_MODEL, D_MODEL)), const((PLE_DIM, D_MODEL)), const((1, D_MODEL))],
        out_specs=full,
        out_shape=jax.ShapeDtypeStruct((t_total, D_MODEL), F32),
        compiler_params=_cparams("parallel"),
        name="ple",
    )(h, p, wt['ple_norm'], wt['ple_gate'], wt['ple_proj'], wt['norm_final'])


def _prep_weights(p):
    row = lambda a: a.reshape(1, -1)
    lane_head = jnp.arange(RWKV_W) // RWKV_N
    return {
        'norm_mix': row(p['norm_mix'][0]),
        'w_ret': p['w_in'][0][:, :RET_PROJ].astype(BF16),
        'w_rw': p['w_in'][0][:, RET_PROJ:].astype(BF16),
        'mu': row(p['rw_mu'][0]), 'w0': row(p['rw_w0'][0]), 'w2': p['rw_w2'][0],
        'a0': row(p['rw_a0'][0]), 'a2': p['rw_a2'][0], 'g2': p['rw_g2'][0],
        'kk': row(p['rw_kk'][0]), 'ka': row(p['rw_ka'][0]), 'rk': row(p['rw_rk'][0]),
        'ln_g': row(p['rw_ln_g'][0]), 'ln_b': row(p['rw_ln_b'][0]),
        'head_ones': (lane_head[:, None] == lane_head[None, :]).astype(F32),
        'w_out_top': p['w_out'][0][:RET_W].astype(BF16),
        'w_out_bot': p['w_out'][0][RET_W:].astype(BF16),
        'norm_ffn': row(p['norm_ffn'][0]),
        'wqt': jnp.stack(_split(p['peer_wq'][0].T)),
        'keys': jnp.stack(_split(p['peer_keys'][0].reshape(2 * PEER_HEADS, PEER_NKEYS, PEER_DKEY // 2))),
        'u': p['peer_u'][0].astype(BF16),
        'vt': p['peer_v'][0].astype(BF16).T,
        'ple_norm': row(p['ple_norm'][0]),
        'ple_gate': p['ple_gate'][0].astype(BF16),
        'ple_proj': p['ple_proj'][0].astype(BF16),
        'norm_final': row(p['norm_final']),
    }


def _chains_to_lanes(a, split_values):
    b, l, _ = a.shape
    a = a.reshape(b, l, RWKV_HEADS, RWKV_N)
    if b * RWKV_HEADS * 2 == LANES:
        if split_values:
            a = a.reshape(b, l, RWKV_HEADS, 2, RWKV_N // 2).transpose(1, 4, 3, 0, 2)
            return a.reshape(1, l, RWKV_N // 2, LANES)
        a = a.transpose(1, 3, 0, 2).reshape(l, RWKV_N, LANES // 2)
        return jnp.concatenate([a, a], axis=-1)[None]
    assert b == LANES
    return a.transpose(2, 1, 3, 0)


def _values_from_lanes(y, b):
    g, l, ni, _ = y.shape
    if g == 1:
        y = y.reshape(l, ni, 2, b, RWKV_HEADS).transpose(3, 0, 4, 2, 1)
    else:
        y = y.transpose(3, 1, 0, 2)
    return y.reshape(b, l, RWKV_W)


def _state_to_lanes(s):
    b = s.shape[0]
    if b * RWKV_HEADS * 2 == LANES:
        s = s.reshape(b, RWKV_HEADS, 2, RWKV_N // 2, RWKV_N).transpose(3, 4, 2, 0, 1)
        return s.reshape(1, RWKV_N // 2, RWKV_N, LANES)
    return s.transpose(1, 2, 3, 0)


def _state_from_lanes(s, b):
    g, ni, _, _ = s.shape
    if g == 1:
        s = s.reshape(ni, RWKV_N, 2, b, RWKV_HEADS).transpose(3, 4, 2, 0, 1)
        return s.reshape(b, RWKV_HEADS, RWKV_N, RWKV_N)
    return s.transpose(3, 0, 1, 2)


def _layer(x, p_l, pos, s_ret, s_wkv, s_shift, wt):
    b, l, _ = x.shape
    x2 = x.reshape(b * l, D_MODEL)
    proj_ret, proj_rw = _inproj(x2, wt['norm_mix'], wt['w_ret'], wt['w_rw'])
    o_ret, s_ret_new = _retention(proj_ret.reshape(b, l, RET_PROJ), s_ret, pos)
    proj_rw = proj_rw.reshape(b, l, RWKV_PROJ)
    new_shift = proj_rw[:, -1]
    r, w, k, kk, cc, v, gate, bonus = _rwkv_prep(proj_rw, s_shift, wt)
    y, s_lanes = _wkv_scan(*(_chains_to_lanes(a, False) for a in (kk, w, cc, k, r)),
                           _chains_to_lanes(v, True), _state_to_lanes(s_wkv))
    y = _values_from_lanes(y, b)
    s_wkv_new = _state_from_lanes(s_lanes, b)
    flat = lambda a: a.reshape(b * l, -1)
    h = _outproj(x2, flat(o_ret), flat(y), flat(bonus), flat(gate), wt)
    h = _peer(h, wt['norm_ffn'], wt['wqt'], wt['keys'], wt['u'], wt['vt'])
    y_out = _ple(h, p_l.reshape(b * l, PLE_DIM), wt)
    return y_out.reshape(b, l, D_MODEL), s_ret_new, s_wkv_new, new_shift


def kernel(x_prompt, x_sample, state_ret, state_wkv, state_shift, p_prompt, p_sample, norm_mix, w_in, rw_mu, rw_w0, rw_w2, rw_a0, rw_a2, rw_g2, rw_kk, rw_ka, rw_rk, rw_ln_g, rw_ln_b, w_out, norm_ffn, peer_wq, peer_keys, peer_u, peer_v, ple_norm, ple_gate, ple_proj, norm_final):
    assert norm_mix.shape[0] == 1, "single-layer trunk"
    wt = _prep_weights(dict(
        norm_mix=norm_mix, w_in=w_in, rw_mu=rw_mu, rw_w0=rw_w0, rw_w2=rw_w2, rw_a0=rw_a0, rw_a2=rw_a2,
        rw_g2=rw_g2, rw_kk=rw_kk, rw_ka=rw_ka, rw_rk=rw_rk, rw_ln_g=rw_ln_g, rw_ln_b=rw_ln_b, w_out=w_out,
        norm_ffn=norm_ffn, peer_wq=peer_wq, peer_keys=peer_keys, peer_u=peer_u, peer_v=peer_v,
        ple_norm=ple_norm, ple_gate=ple_gate, ple_proj=ple_proj, norm_final=norm_final))
    bp, lp, _ = x_prompt.shape
    bs, ls, _ = x_sample.shape
    pos_p = jnp.arange(lp, dtype=jnp.int32)
    pos_s = PAST_LEN + jnp.arange(ls, dtype=jnp.int32)
    zero_ret = jnp.zeros((bp, RET_HEADS, RET_DK, RET_DV), F32)
    zero_wkv = jnp.zeros((bp, RWKV_HEADS, RWKV_N, RWKV_N), F32)
    zero_shift = jnp.zeros((bp, RWKV_PROJ), F32)
    yp, rp, wp, sp = _layer(x_prompt, p_prompt[0], pos_p, zero_ret, zero_wkv, zero_shift, wt)
    ys, rs, ws, ss = _layer(x_sample, p_sample[0], pos_s, state_ret[0], state_wkv[0], state_shift[0], wt)
    return (yp, ys, rp[None], wp[None], sp[None], rs[None], ws[None], ss[None])
```

```python
import functools
import math

import jax
import jax.numpy as jnp
from jax import lax
from jax.experimental import pallas as pl
from jax.experimental.pallas import tpu as pltpu

F32 = jnp.float32
BF16 = jnp.bfloat16

D_MODEL = 1024
PAST_LEN = 16384
RET_HEADS = 8
RET_DK = 64
RET_DV = 64
RET_W = RET_HEADS * RET_DV
RWKV_HEADS = 8
RWKV_N = 64
RWKV_W = RWKV_HEADS * RWKV_N
DECAY_LORA = 64
AAA_LORA = 64
GATE_LORA = 128
RET_PROJ = 2 * RET_HEADS * RET_DK + 2 * RET_W
RWKV_PROJ = 3 * RWKV_W + DECAY_LORA + AAA_LORA + GATE_LORA
RET_CHUNK = 128
ROPE_BASE = 10000.0
PEER_HEADS = 8
PEER_NKEYS = 128
PEER_EXPERTS = PEER_NKEYS * PEER_NKEYS
PEER_DKEY = 256
PEER_TOPK = 16
PLE_DIM = 256
RMS_EPS = 1e-6
GN_EPS = 1e-5
RWKV_LN_EPS = 64e-5

LANES = 128
SUBLANES = 8
BF16_ROWS = 16
VMEM_LIMIT = 56 * 1024 * 1024

NEG_INF = float("-inf")


def _cparams(*sem):
    return pltpu.CompilerParams(dimension_semantics=sem, vmem_limit_bytes=VMEM_LIMIT)


def _rms(x, g):
    return x * lax.rsqrt(jnp.mean(x * x, axis=-1, keepdims=True) + RMS_EPS) * g


def _split(x):
    hi = x.astype(BF16)
    return hi, (x - hi.astype(F32)).astype(BF16)


def _dot3(a, b):
    ah, al = _split(a)
    bh, bl = _split(b)
    dot = functools.partial(jnp.dot, preferred_element_type=F32)
    return dot(ah, bh) + dot(ah, bl) + dot(al, bh)


def _head_sums(x, head_ones):
    xh, xl = _split(x)
    dot = functools.partial(jnp.dot, preferred_element_type=F32)
    return dot(xh, head_ones) + dot(xl, head_ones)


def _dot_nt(a, b):
    return lax.dot_general(a, b, (((1,), (1,)), ((), ())), preferred_element_type=F32)


def _dot_tn(a, b):
    return lax.dot_general(a, b, (((0,), (0,)), ((), ())), preferred_element_type=F32)


PEER_TT = 512
PEER_ROWS = 8
PEER_EB = PEER_ROWS * PEER_NKEYS
PEER_CHUNK = 256
PEER_NTOP = PEER_TOPK + 1
_CANDS = tuple((a, b) for a in range(PEER_NTOP) for b in range(PEER_NTOP)
               if (a + 1) * (b + 1) <= PEER_NTOP)


def _peer_kernel(h_ref, g_ref, wqt_ref, keys_ref, u_ref, vt_ref, vlast_ref, o_ref,
                 xn_ref, s_ref, e1_ref, se2_ref, top_ref, tau_ref, acc_ref, ht_ref, act_ref, *, tt):
    eb = pl.program_id(1)
    nlb = tt // LANES
    n_ck = PEER_EB // PEER_CHUNK
    chunk = lambda ck: slice(ck * PEER_CHUNK, (ck + 1) * PEER_CHUNK)

    @pl.when(eb == 0)
    def _prepare():
        xn = _rms(h_ref[...], g_ref[...])
        xh, xl = _split(xn)
        xn_ref[...] = xh
        qt = (_dot_nt(wqt_ref[0], xh) + _dot_nt(wqt_ref[0], xl) + _dot_nt(wqt_ref[1], xh))
        for hc in range(2 * PEER_HEADS):
            qh, ql = _split(qt[hc * PEER_NKEYS:(hc + 1) * PEER_NKEYS, :])
            s_ref[hc] = (jnp.dot(keys_ref[0, hc], qh, preferred_element_type=F32)
                         + jnp.dot(keys_ref[0, hc], ql, preferred_element_type=F32)
                         + jnp.dot(keys_ref[1, hc], qh, preferred_element_type=F32))

        def per_lane_block(lb, carry):
            lanes = pl.ds(pl.multiple_of(lb * LANES, LANES), LANES)
            for hc in range(2 * PEER_HEADS):
                h, c = divmod(hc, 2)
                work = s_ref[hc, :, lanes]
                ex = jnp.exp(work - jnp.max(work, axis=0, keepdims=True))
                if c == 0:
                    e1_ref[h, :, lanes] = ex
                else:
                    for rb in range(PEER_NKEYS // SUBLANES):
                        rows = slice(rb * SUBLANES, (rb + 1) * SUBLANES)
                        se2_ref[lb, rb, 2 * h] = work[rows]
                        se2_ref[lb, rb, 2 * h + 1] = ex[rows]
                for r in range(PEER_NTOP):
                    m = jnp.max(work, axis=0, keepdims=True)
                    top_ref[c, r, pl.ds(h, 1), lanes] = m
                    work = jnp.where(work == m, NEG_INF, work)
            v1 = [top_ref[0, r, :, lanes] for r in range(PEER_NTOP)]
            v2 = [top_ref[1, r, :, lanes] for r in range(PEER_NTOP)]
            cands = [v1[a] + v2[b] for (a, b) in _CANDS]
            cv = []
            for r in range(PEER_TOPK + 1):
                m = functools.reduce(jnp.maximum, cands)
                cv.append(m)
                if r < PEER_TOPK:
                    cands = [jnp.where(x == m, NEG_INF, x) for x in cands]
            z = functools.reduce(lambda a, b: a + b, [jnp.exp(x - cv[0]) for x in cv[:PEER_TOPK]])
            tau_ref[0, :, lanes] = 0.5 * (cv[PEER_TOPK - 1] + cv[PEER_TOPK])
            tau_ref[1, :, lanes] = 1.0 / z
            return carry

        lax.fori_loop(0, nlb, per_lane_block, 0)
        acc_ref[...] = jnp.zeros_like(acc_ref)
        act_ref[chunk(n_ck - 1), :] = jnp.zeros((PEER_CHUNK, tt), BF16)

    irows =pl.ds(pl.multiple_of(eb * PEER_ROWS, PEER_ROWS), PEER_ROWS)
    shape = (SUBLANES, LANES)

    def gate_rows(il, lb):
        lanes = slice(lb * LANES, (lb + 1) * LANES)
        tau = tau_ref[0, :, lanes]
        rz = tau_ref[1, :, lanes]
        theta, e1 = [], []
        for h in range(PEER_HEADS):
            theta.append(jnp.broadcast_to(tau[h:h + 1] - s_ref[2 * h, irows, lanes][il:il + 1], shape))
            e1.append(jnp.broadcast_to(e1_ref[h, irows, lanes][il:il + 1] * rz[h:h + 1], shape))
        for jb in range(PEER_NKEYS // BF16_ROWS):
            pieces = []
            for half in range(2):
                rb = 2 * jb + half
                j0 = rb * SUBLANES
                terms = [jnp.where(se2_ref[lb, rb, 2 * h] >= theta[h],
                                   e1[h] * se2_ref[lb, rb, 2 * h + 1], 0.0) for h in range(PEER_HEADS)]
                while len(terms) > 1:
                    terms = [terms[i] + terms[i + 1] for i in range(0, len(terms), 2)]
                r0 = il * PEER_NKEYS + j0
                hv = ht_ref[r0:r0 + SUBLANES, lanes]
                pieces.append(0.5 * hv * (1.0 + lax.erf(hv * (2.0 ** -0.5))) * terms[0])
            a0 = il * PEER_NKEYS + jb * BF16_ROWS
            act_ref[a0:a0 + BF16_ROWS, lanes] = jnp.concatenate(pieces, axis=0).astype(BF16)

    last = n_ck - 1

    def hval(ck):
        ht_ref[chunk(ck), :] = _dot_nt(u_ref[chunk(ck), :], xn_ref[...])

    def vacc(v_cols, ck):
        acc_ref[...] += jnp.dot(v_cols, act_ref[chunk(ck), :], preferred_element_type=F32)

    hval(0)
    hval(1)
    vacc(vlast_ref[...], last)
    for ck in range(2, n_ck):
        hval(ck)
    for ck in range(n_ck):
        for il in range(ck * PEER_CHUNK // PEER_NKEYS, (ck + 1) * PEER_CHUNK // PEER_NKEYS):
            for lb in range(nlb):
                gate_rows(il, lb)
        if ck < last:
            vacc(vt_ref[:, chunk(ck)], ck)

    @pl.when(eb == pl.num_programs(1) - 1)
    def _finish():
        tail = jnp.dot(vt_ref[:, chunk(last)], act_ref[chunk(last), :], preferred_element_type=F32)
        o_ref[...] = h_ref[...] + (acc_ref[...] + tail).T


def _peer(h, norm_g, wqt, keys, u, vt):
    t_total = h.shape[0]
    tt = min(PEER_TT, t_total)
    assert t_total % tt == 0 and tt % LANES == 0
    n_eb = PEER_EXPERTS // PEER_EB
    n_ck = PEER_EB // PEER_CHUNK
    kern = functools.partial(_peer_kernel, tt=tt)
    return pl.pallas_call(
        kern,
        grid=(t_total // tt, n_eb),
        in_specs=[
            pl.BlockSpec((tt, D_MODEL), lambda t, e: (t, 0)),
            pl.BlockSpec((1, D_MODEL), lambda t, e: (0, 0)),
            pl.BlockSpec((2, PEER_HEADS * PEER_DKEY, D_MODEL), lambda t, e: (0, 0, 0)),
            pl.BlockSpec((2, 2 * PEER_HEADS, PEER_NKEYS, PEER_DKEY // 2), lambda t, e: (0, 0, 0, 0)),
            pl.BlockSpec((PEER_EB, D_MODEL), lambda t, e: (e, 0)),
            pl.BlockSpec((D_MODEL, PEER_EB), lambda t, e: (0, e)),
            pl.BlockSpec((D_MODEL, PEER_CHUNK), lambda t, e: (0, jnp.maximum(e * n_ck - 1, 0))),
        ],
        out_specs=pl.BlockSpec((tt, D_MODEL), lambda t, e: (t, 0)),
        out_shape=jax.ShapeDtypeStruct((t_total, D_MODEL), F32),
        scratch_shapes=[
            pltpu.VMEM((tt, D_MODEL), BF16),
            pltpu.VMEM((2 * PEER_HEADS, PEER_NKEYS, tt), F32),
            pltpu.VMEM((PEER_HEADS, PEER_NKEYS, tt), F32),
            pltpu.VMEM((tt // LANES, PEER_NKEYS // SUBLANES, 2 * PEER_HEADS, SUBLANES, LANES), F32),
            pltpu.VMEM((2, PEER_NTOP, PEER_HEADS, tt), F32),
            pltpu.VMEM((2, PEER_HEADS, tt), F32),
            pltpu.VMEM((D_MODEL, tt), F32),
            pltpu.VMEM((PEER_EB, tt), F32),
            pltpu.VMEM((PEER_EB, tt), BF16),
        ],
        compiler_params=_cparams("parallel", "arbitrary"),
        name="peer",
    )(h, norm_g, wqt, keys, u, vt, vt)


TOKEN_TILE = 256


def _inproj_kernel(x_ref, g_ref, wr_ref, ww_ref, pr_ref, pw_ref):
    xn = _rms(x_ref[...], g_ref[...]).astype(BF16)
    pr_ref[...] = jnp.dot(xn, wr_ref[...], preferred_element_type=F32)
    pw_ref[...] = jnp.dot(xn, ww_ref[...], preferred_element_type=F32)


def _inproj(x, norm_g, w_ret, w_rw):
    t_total = x.shape[0]
    tm = min(TOKEN_TILE, t_total)
    assert t_total % tm == 0
    const = lambda shape: pl.BlockSpec(shape, lambda t: (0,) * len(shape))
    return pl.pallas_call(
        _inproj_kernel,
        grid=(t_total // tm,),
        in_specs=[pl.BlockSpec((tm, D_MODEL), lambda t: (t, 0)), const((1, D_MODEL)),
                  const((D_MODEL, RET_PROJ)), const((D_MODEL, RWKV_PROJ))],
        out_specs=[pl.BlockSpec((tm, RET_PROJ), lambda t: (t, 0)),
                   pl.BlockSpec((tm, RWKV_PROJ), lambda t: (t, 0))],
        out_shape=[jax.ShapeDtypeStruct((t_total, RET_PROJ), F32),
                   jax.ShapeDtypeStruct((t_total, RWKV_PROJ), F32)],
        compiler_params=_cparams("parallel"),
        name="inproj",
    )(x, norm_g, w_ret, w_rw)


def _rope_partner(x):
    lane = lax.broadcasted_iota(jnp.int32, x.shape, 1)
    up = pltpu.roll(x, x.shape[1] - RET_DK // 2, 1)
    down = pltpu.roll(x, RET_DK // 2, 1)
    return jnp.where(lane % RET_DK < RET_DK // 2, up, down)


def _retention_kernel(p_ref, s0_ref, cos_ref, sin_ref, dmat_ref, xz_ref, ones_ref, o_ref, so_ref, s_ref,
                      *, nb, g_chunk):
    ci = pl.program_id(1)
    qk = RET_HEADS * RET_DK

    @pl.when(ci == 0)
    def _init():
        s_ref[...] = s0_ref[...]

    cos = cos_ref[...]
    sin = sin_ref[...]

    def per_seq(n, carry):
        x = p_ref[n]
        q = x[:, :qk]
        k = x[:, qk:2 * qk]
        q = q * cos + _rope_partner(q) * sin
        k = (k * cos + _rope_partner(k) * sin) * (RET_DK ** -0.5)
        for h in range(RET_HEADS):
            sl = slice(h * RET_DK, (h + 1) * RET_DK)
            qh, kh = q[:, sl], k[:, sl]
            vh = x[:, 2 * qk + h * RET_DV:2 * qk + (h + 1) * RET_DV]
            s_h = s_ref[n, h]
            xi = xz_ref[h, :, 0:1]
            zeta = xz_ref[h, :, 1:2]
            qb, kb, vb = qh.astype(BF16), kh.astype(BF16), vh.astype(BF16)
            inner = _dot_nt(qb, kb) * dmat_ref[h]
            o = (jnp.dot(inner.astype(BF16), vb, preferred_element_type=F32)
                 + jnp.dot(qb, s_h.astype(BF16), preferred_element_type=F32) * xi)
            s_ref[n, h] = s_h * g_chunk[h] + _dot_tn((kh * zeta).astype(BF16), vb)
            o_ref[n, :, sl] = o
        o = o_ref[n]
        g = x[:, 2 * qk + RET_W:]
        oc = o - _head_sums(o, ones_ref[...]) * (1.0 / RET_DV)
        var = _head_sums(oc * oc, ones_ref[...]) * (1.0 / RET_DV)
        o_ref[n] = oc * lax.rsqrt(var + GN_EPS) * (g * jax.nn.sigmoid(g))
        return carry

    lax.fori_loop(0, nb, per_seq, 0)

    @pl.when(ci == pl.num_programs(1) - 1)
    def _done():
        so_ref[...] = s_ref[...]


def _retention(proj_ret, s0, pos, head_ones):
    assert RET_W == RWKV_W and RET_DV == RWKV_N
    b, l, _ = proj_ret.shape
    c = RET_CHUNK if l % RET_CHUNK == 0 else l
    nb = 1 if c == RET_CHUNK else 16
    assert b % nb == 0
    half = RET_DK // 2
    inv = ROPE_BASE ** (-jnp.arange(half, dtype=F32) / half)
    ang = pos.astype(F32)[:, None] * inv[None, :]
    cos = jnp.tile(jnp.cos(ang), (1, 2 * RET_HEADS))
    sin = jnp.tile(jnp.concatenate([-jnp.sin(ang), jnp.sin(ang)], axis=1), (1, RET_HEADS))
    lg = jnp.log(1.0 - 2.0 ** (-5.0 - jnp.arange(RET_HEADS, dtype=F32)))
    idx = jnp.arange(c, dtype=F32)
    diff = idx[:, None] - idx[None, :]
    dmat = jnp.where(diff[None] >= 0, jnp.exp(jnp.maximum(diff, 0.0)[None] * lg[:, None, None]), 0.0)
    xi = jnp.exp((idx + 1.0)[None, :] * lg[:, None])
    zeta = jnp.exp((c - 1.0 - idx)[None, :] * lg[:, None])
    xz = jnp.stack([xi, zeta], axis=-1)
    g_chunk = tuple(math.exp(c * math.log(1.0 - 2.0 ** (-5.0 - h))) for h in range(RET_HEADS))
    kern = functools.partial(_retention_kernel, nb=nb, g_chunk=g_chunk)
    return pl.pallas_call(
        kern,
        grid=(b // nb, l // c),
        in_specs=[
            pl.BlockSpec((nb, c, RET_PROJ), lambda i, j: (i, j, 0)),
            pl.BlockSpec((nb, RET_HEADS, RET_DK, RET_DV), lambda i, j: (i, 0, 0, 0)),
            pl.BlockSpec((c, RET_HEADS * RET_DK), lambda i, j: (j, 0)),
            pl.BlockSpec((c, RET_HEADS * RET_DK), lambda i, j: (j, 0)),
            pl.BlockSpec((RET_HEADS, c, c), lambda i, j: (0, 0, 0)),
            pl.BlockSpec((RET_HEADS, c, 2), lambda i, j: (0, 0, 0)),
            pl.BlockSpec((RET_W, RET_W), lambda i, j: (0, 0)),
        ],
        out_specs=[pl.BlockSpec((nb, c, RET_W), lambda i, j: (i, j, 0)),
                   pl.BlockSpec((nb, RET_HEADS, RET_DK, RET_DV), lambda i, j: (i, 0, 0, 0))],
        out_shape=[jax.ShapeDtypeStruct((b, l, RET_W), F32),
                   jax.ShapeDtypeStruct((b, RET_HEADS, RET_DK, RET_DV), F32)],
        scratch_shapes=[pltpu.VMEM((nb, RET_HEADS, RET_DK, RET_DV), F32)],
        compiler_params=_cparams("parallel", "arbitrary"),
        name="retention",
    )(proj_ret, s0, cos, sin, dmat, xz, head_ones)


def _softplus(x):
    return jnp.maximum(x, 0.0) + jnp.log1p(jnp.exp(-jnp.abs(x)))


def _rwkv_prep_kernel(x_ref, sh_ref, mu_ref, w0_ref, w2_ref, a0_ref, a2_ref, g2_ref, kks_ref, ka_ref,
                      rk_ref, ones_ref, r_ref, w_ref, k_ref, kk_ref, cc_ref, v_ref, gate_ref, bonus_ref,
                      carry_ref, *, nb, tb):
    ti = pl.program_id(1)
    x = x_ref[...]
    first = jnp.where(ti == 0, sh_ref[...], carry_ref[...])
    prev = pltpu.roll(x.reshape(nb * tb, RWKV_PROJ), 1, 0).reshape(nb, tb, RWKV_PROJ)
    tpos = lax.broadcasted_iota(jnp.int32, x.shape, 1)
    prev = jnp.where(tpos == 0, first, prev)
    carry_ref[...] = x[:, tb - 1:tb, :]
    f = (x + (prev - x) * mu_ref[...]).reshape(nb * tb, RWKV_PROJ)
    o1, o2, o3 = RWKV_W, 2 * RWKV_W, 3 * RWKV_W
    r, k7, v7 = f[:, :o1], f[:, o1:o2], f[:, o2:o3]
    fw = f[:, o3:o3 + DECAY_LORA]
    fa = f[:, o3 + DECAY_LORA:o3 + DECAY_LORA + AAA_LORA]
    fg = f[:, o3 + DECAY_LORA + AAA_LORA:]
    w = -_softplus(-(w0_ref[...] + _dot3(jnp.tanh(fw), w2_ref[...]))) - 0.5
    a = jax.nn.sigmoid(a0_ref[...] + _dot3(fa, a2_ref[...]))
    gate = _dot3(jax.nn.sigmoid(fg), g2_ref[...])
    kk = k7 * kks_ref[...]
    kk = kk / jnp.maximum(jnp.sqrt(_head_sums(kk * kk, ones_ref[...])), 1e-12)
    k = k7 * (1.0 + (a - 1.0) * ka_ref[...])
    out3 = lambda y: y.reshape(nb, tb, RWKV_W)
    r_ref[...] = out3(r)
    w_ref[...] = out3(jnp.exp(-jnp.exp(w)))
    k_ref[...] = out3(k)
    kk_ref[...] = out3(kk)
    cc_ref[...] = out3(kk * a)
    v_ref[...] = out3(v7)
    gate_ref[...] = out3(gate)
    bonus_ref[...] = out3(_head_sums(r * k * rk_ref[...], ones_ref[...]) * v7)


def _rwkv_prep(proj_rw, shift, wt):
    b, l, _ = proj_rw.shape
    tb = min(TOKEN_TILE, l)
    nb = max(1, 128 // tb)
    assert l % tb == 0 and b % nb == 0 and tb % SUBLANES == 0
    const = lambda shape: pl.BlockSpec(shape, lambda i, j: (0,) * len(shape))
    blk = pl.BlockSpec((nb, tb, RWKV_W), lambda i, j: (i, j, 0))
    kern = functools.partial(_rwkv_prep_kernel, nb=nb, tb=tb)
    return pl.pallas_call(
        kern,
        grid=(b // nb, l // tb),
        in_specs=[
            pl.BlockSpec((nb, tb, RWKV_PROJ), lambda i, j: (i, j, 0)),
            pl.BlockSpec((nb, 1, RWKV_PROJ), lambda i, j: (i, 0, 0)),
            const((1, RWKV_PROJ)), const((1, RWKV_W)), const((DECAY_LORA, RWKV_W)),
            const((1, RWKV_W)), const((AAA_LORA, RWKV_W)), const((GATE_LORA, RWKV_W)),
            const((1, RWKV_W)), const((1, RWKV_W)), const((1, RWKV_W)), const((RWKV_W, RWKV_W)),
        ],
        out_specs=[blk] * 8,
        out_shape=[jax.ShapeDtypeStruct((b, l, RWKV_W), F32)] * 8,
        scratch_shapes=[pltpu.VMEM((nb, 1, RWKV_PROJ), F32)],
        compiler_params=_cparams("parallel", "arbitrary"),
        name="rwkv_prep",
    )(proj_rw, shift.reshape(b, 1, RWKV_PROJ), wt['mu'], wt['w0'], wt['w2'], wt['a0'], wt['a2'],
      wt['g2'], wt['kk'], wt['ka'], wt['rk'], wt['head_ones'])


def _wkv_kernel(kk_ref, w_ref, cc_ref, k_ref, r_ref, v_ref, s0_ref, y_ref, so_ref, s_ref, *, tb, ni):
    ti = pl.program_id(1)

    @pl.when(ti == 0)
    def _init():
        s_ref[...] = s0_ref[0]

    def step(t, carry):
        kk, w, cc, k, r = kk_ref[0, t], w_ref[0, t], cc_ref[0, t], k_ref[0, t], r_ref[0, t]
        for ii in range(ni):
            s = s_ref[ii]
            sa = -jnp.sum(s * kk, axis=0, keepdims=True)
            sn = s * w + sa * cc + v_ref[0, t, ii:ii + 1, :] * k
            s_ref[ii] = sn
            y_ref[0, t, ii:ii + 1, :] = jnp.sum(sn * r, axis=0, keepdims=True)
        return carry

    lax.fori_loop(0, tb, step, 0)

    @pl.when(ti == pl.num_programs(1) - 1)
    def _done():
        so_ref[0] = s_ref[...]


def _wkv_scan(kk, w, cc, k, r, v, s0):
    g, l, ni, _ = v.shape
    tb = min(32, l)
    assert l % tb == 0
    jblk = pl.BlockSpec((1, tb, RWKV_N, LANES), lambda i, j: (i, j, 0, 0))
    iblk = pl.BlockSpec((1, tb, ni, LANES), lambda i, j: (i, j, 0, 0))
    sblk = pl.BlockSpec((1, ni, RWKV_N, LANES), lambda i, j: (i, 0, 0, 0))
    kern = functools.partial(_wkv_kernel, tb=tb, ni=ni)
    return pl.pallas_call(
        kern,
        grid=(g, l // tb),
        in_specs=[jblk] * 5 + [iblk, sblk],
        out_specs=[iblk, sblk],
        out_shape=[jax.ShapeDtypeStruct((g, l, ni, LANES), F32),
                   jax.ShapeDtypeStruct((g, ni, RWKV_N, LANES), F32)],
        scratch_shapes=[pltpu.VMEM((ni, RWKV_N, LANES), F32)],
        compiler_params=_cparams("parallel", "arbitrary"),
        name="wkv_scan",
    )(kk, w, cc, k, r, v, s0)


def _outproj_kernel(x_ref, oret_ref, y_ref, bonus_ref, gate_ref, lng_ref, lnb_ref, ones_ref,
                    wtop_ref, wbot_ref, o_ref):
    y = y_ref[...]
    yc = y - _head_sums(y, ones_ref[...]) * (1.0 / RWKV_N)
    var = _head_sums(yc * yc, ones_ref[...]) * (1.0 / RWKV_N)
    o = yc * lax.rsqrt(var + RWKV_LN_EPS) * lng_ref[...] + lnb_ref[...]
    o = (o + bonus_ref[...]) * gate_ref[...]
    o_ref[...] = (x_ref[...]
                  + jnp.dot(oret_ref[...].astype(BF16), wtop_ref[...], preferred_element_type=F32)
                  + jnp.dot(o.astype(BF16), wbot_ref[...], preferred_element_type=F32))


def _outproj(x, o_ret, y, bonus, gate, wt):
    t_total = x.shape[0]
    tm = min(TOKEN_TILE, t_total)
    const = lambda shape: pl.BlockSpec(shape, lambda t: (0,) * len(shape))
    half = pl.BlockSpec((tm, RWKV_W), lambda t: (t, 0))
    full = pl.BlockSpec((tm, D_MODEL), lambda t: (t, 0))
    return pl.pallas_call(
        _outproj_kernel,
        grid=(t_total // tm,),
        in_specs=[full, half, half, half, half, const((1, RWKV_W)), const((1, RWKV_W)),
                  const((RWKV_W, RWKV_W)), const((RET_W, D_MODEL)), const((RWKV_W, D_MODEL))],
        out_specs=full,
        out_shape=jax.ShapeDtypeStruct((t_total, D_MODEL), F32),
        compiler_params=_cparams("parallel"),
        name="outproj",
    )(x, o_ret, y, bonus, gate, wt['ln_g'], wt['ln_b'], wt['head_ones'], wt['w_out_top'], wt['w_out_bot'])


def _ple_kernel(h_ref, p_ref, gn_ref, wg_ref, wp_ref, fn_ref, o_ref):
    h = h_ref[...]
    gate = jax.nn.sigmoid(jnp.dot(_rms(h, gn_ref[...]).astype(BF16), wg_ref[...], preferred_element_type=F32))
    h = h + gate * jnp.dot(p_ref[...].astype(BF16), wp_ref[...], preferred_element_type=F32)
    o_ref[...] = _rms(h, fn_ref[...])


def _ple(h, p, wt):
    t_total = h.shape[0]
    tm = min(TOKEN_TILE, t_total)
    const = lambda shape: pl.BlockSpec(shape, lambda t: (0,) * len(shape))
    full = pl.BlockSpec((tm, D_MODEL), lambda t: (t, 0))
    return pl.pallas_call(
        _ple_kernel,
        grid=(t_total // tm,),
        in_specs=[full, pl.BlockSpec((tm, PLE_DIM), lambda t: (t, 0)), const((1, D_MODEL)),
                  const((D_MODEL, D_MODEL)), const((PLE_DIM, D_MODEL)), const((1, D_MODEL))],
        out_specs=full,
        out_shape=jax.ShapeDtypeStruct((t_total, D_MODEL), F32),
        compiler_params=_cparams("parallel"),
        name="ple",
    )(h, p, wt['ple_norm'], wt['ple_gate'], wt['ple_proj'], wt['norm_final'])


def _prep_weights(p):
    row = lambda a: a.reshape(1, -1)
    lane_head = jnp.arange(RWKV_W) // RWKV_N
    return {
        'norm_mix': row(p['norm_mix'][0]),
        'w_ret': p['w_in'][0][:, :RET_PROJ].astype(BF16),
        'w_rw': p['w_in'][0][:, RET_PROJ:].astype(BF16),
        'mu': row(p['rw_mu'][0]), 'w0': row(p['rw_w0'][0]), 'w2': p['rw_w2'][0],
        'a0': row(p['rw_a0'][0]), 'a2': p['rw_a2'][0], 'g2': p['rw_g2'][0],
        'kk': row(p['rw_kk'][0]), 'ka': row(p['rw_ka'][0]), 'rk': row(p['rw_rk'][0]),
        'ln_g': row(p['rw_ln_g'][0]), 'ln_b': row(p['rw_ln_b'][0]),
        'head_ones': (lane_head[:, None] == lane_head[None, :]).astype(BF16),
        'w_out_top': p['w_out'][0][:RET_W].astype(BF16),
        'w_out_bot': p['w_out'][0][RET_W:].astype(BF16),
        'norm_ffn': row(p['norm_ffn'][0]),
        'wqt': jnp.stack(_split(p['peer_wq'][0].T)),
        'keys': jnp.stack(_split(p['peer_keys'][0].reshape(2 * PEER_HEADS, PEER_NKEYS, PEER_DKEY // 2))),
        'u': p['peer_u'][0].astype(BF16),
        'vt': p['peer_v'][0].astype(BF16).T,
        'ple_norm': row(p['ple_norm'][0]),
        'ple_gate': p['ple_gate'][0].astype(BF16),
        'ple_proj': p['ple_proj'][0].astype(BF16),
        'norm_final': row(p['norm_final']),
    }


def _chains_to_lanes(a, split_values):
    b, l, _ = a.shape
    a = a.reshape(b, l, RWKV_HEADS, RWKV_N)
    if b * RWKV_HEADS * 2 == LANES:
        if split_values:
            a = a.reshape(b, l, RWKV_HEADS, 2, RWKV_N // 2).transpose(1, 4, 3, 0, 2)
            return a.reshape(1, l, RWKV_N // 2, LANES)
        a = a.transpose(1, 3, 0, 2).reshape(l, RWKV_N, LANES // 2)
        return jnp.concatenate([a, a], axis=-1)[None]
    assert b == LANES
    return a.transpose(2, 1, 3, 0)


def _values_from_lanes(y, b):
    g, l, ni, _ = y.shape
    if g == 1:
        y = y.reshape(l, ni, 2, b, RWKV_HEADS).transpose(3, 0, 4, 2, 1)
    else:
        y = y.transpose(3, 1, 0, 2)
    return y.reshape(b, l, RWKV_W)


def _state_to_lanes(s):
    b = s.shape[0]
    if b * RWKV_HEADS * 2 == LANES:
        s = s.reshape(b, RWKV_HEADS, 2, RWKV_N // 2, RWKV_N).transpose(3, 4, 2, 0, 1)
        return s.reshape(1, RWKV_N // 2, RWKV_N, LANES)
    return s.transpose(1, 2, 3, 0)


def _state_from_lanes(s, b):
    g, ni, _, _ = s.shape
    if g == 1:
        s = s.reshape(ni, RWKV_N, 2, b, RWKV_HEADS).transpose(3, 4, 2, 0, 1)
        return s.reshape(b, RWKV_HEADS, RWKV_N, RWKV_N)
    return s.transpose(3, 0, 1, 2)


def _layer(x, p_l, pos, s_ret, s_wkv, s_shift, wt):
    b, l, _ = x.shape
    x2 = x.reshape(b * l, D_MODEL)
    proj_ret, proj_rw = _inproj(x2, wt['norm_mix'], wt['w_ret'], wt['w_rw'])
    o_ret, s_ret_new = _retention(proj_ret.reshape(b, l, RET_PROJ), s_ret, pos, wt['head_ones'])
    proj_rw = proj_rw.reshape(b, l, RWKV_PROJ)
    new_shift = proj_rw[:, -1]
    r, w, k, kk, cc, v, gate, bonus = _rwkv_prep(proj_rw, s_shift, wt)
    y, s_lanes = _wkv_scan(*(_chains_to_lanes(a, False) for a in (kk, w, cc, k, r)),
                           _chains_to_lanes(v, True), _state_to_lanes(s_wkv))
    y = _values_from_lanes(y, b)
    s_wkv_new = _state_from_lanes(s_lanes, b)
    flat = lambda a: a.reshape(b * l, -1)
    h = _outproj(x2, flat(o_ret), flat(y), flat(bonus), flat(gate), wt)
    h = _peer(h, wt['norm_ffn'], wt['wqt'], wt['keys'], wt['u'], wt['vt'])
    y_out = _ple(h, p_l.reshape(b * l, PLE_DIM), wt)
    return y_out.reshape(b, l, D_MODEL), s_ret_new, s_wkv_new, new_shift


def kernel(x_prompt, x_sample, state_ret, state_wkv, state_shift, p_prompt, p_sample, norm_mix, w_in, rw_mu, rw_w0, rw_w2, rw_a0, rw_a2, rw_g2, rw_kk, rw_ka, rw_rk, rw_ln_g, rw_ln_b, w_out, norm_ffn, peer_wq, peer_keys, peer_u, peer_v, ple_norm, ple_gate, ple_proj, norm_final):
    assert norm_mix.shape[0] == 1, "single-layer trunk"
    wt = _prep_weights(dict(
        norm_mix=norm_mix, w_in=w_in, rw_mu=rw_mu, rw_w0=rw_w0, rw_w2=rw_w2, rw_a0=rw_a0, rw_a2=rw_a2,
        rw_g2=rw_g2, rw_kk=rw_kk, rw_ka=rw_ka, rw_rk=rw_rk, rw_ln_g=rw_ln_g, rw_ln_b=rw_ln_b, w_out=w_out,
        norm_ffn=norm_ffn, peer_wq=peer_wq, peer_keys=peer_keys, peer_u=peer_u, peer_v=peer_v,
        ple_norm=ple_norm, ple_gate=ple_gate, ple_proj=ple_proj, norm_final=norm_final))
    bp, lp, _ = x_prompt.shape
    bs, ls, _ = x_sample.shape
    pos_p = jnp.arange(lp, dtype=jnp.int32)
    pos_s = PAST_LEN + jnp.arange(ls, dtype=jnp.int32)
    zero_ret = jnp.zeros((bp, RET_HEADS, RET_DK, RET_DV), F32)
    zero_wkv = jnp.zeros((bp, RWKV_HEADS, RWKV_N, RWKV_N), F32)
    zero_shift = jnp.zeros((bp, RWKV_PROJ), F32)
    yp, rp, wp, sp = _layer(x_prompt, p_prompt[0], pos_p, zero_ret, zero_wkv, zero_shift, wt)
    ys, rs, ws, ss = _layer(x_sample, p_sample[0], pos_s, state_ret[0], state_wkv[0], state_shift[0], wt)
    return (yp, ys, rp[None], wp[None], sp[None], rs[None], ws[None], ss[None])
```

```python
import functools
import math

import jax
import jax.numpy as jnp
from jax import lax
from jax.experimental import pallas as pl
from jax.experimental.pallas import tpu as pltpu

F32 = jnp.float32
BF16 = jnp.bfloat16

D_MODEL = 1024
PAST_LEN = 16384
RET_HEADS = 8
RET_DK = 64
RET_DV = 64
RET_W = RET_HEADS * RET_DV
RWKV_HEADS = 8
RWKV_N = 64
RWKV_W = RWKV_HEADS * RWKV_N
DECAY_LORA = 64
AAA_LORA = 64
GATE_LORA = 128
RET_PROJ = 2 * RET_HEADS * RET_DK + 2 * RET_W
RWKV_PROJ = 3 * RWKV_W + DECAY_LORA + AAA_LORA + GATE_LORA
RET_CHUNK = 128
ROPE_BASE = 10000.0
PEER_HEADS = 8
PEER_NKEYS = 128
PEER_EXPERTS = PEER_NKEYS * PEER_NKEYS
PEER_DKEY = 256
PEER_TOPK = 16
PLE_DIM = 256
RMS_EPS = 1e-6
GN_EPS = 1e-5
RWKV_LN_EPS = 64e-5

LANES = 128
SUBLANES = 8
BF16_ROWS = 16
VMEM_LIMIT = 56 * 1024 * 1024

NEG_INF = float("-inf")


def _cparams(*sem):
    return pltpu.CompilerParams(dimension_semantics=sem, vmem_limit_bytes=VMEM_LIMIT)


def _rms(x, g):
    return x * lax.rsqrt(jnp.mean(x * x, axis=-1, keepdims=True) + RMS_EPS) * g


def _split(x):
    hi = x.astype(BF16)
    return hi, (x - hi.astype(F32)).astype(BF16)


def _dot3(a, b):
    ah, al = _split(a)
    bh, bl = _split(b)
    dot = functools.partial(jnp.dot, preferred_element_type=F32)
    return dot(ah, bh) + dot(ah, bl) + dot(al, bh)


def _head_sums(x, head_ones):
    xh, xl = _split(x)
    dot = functools.partial(jnp.dot, preferred_element_type=F32)
    return dot(xh, head_ones) + dot(xl, head_ones)


def _dot_nt(a, b):
    return lax.dot_general(a, b, (((1,), (1,)), ((), ())), preferred_element_type=F32)


def _dot_tn(a, b):
    return lax.dot_general(a, b, (((0,), (0,)), ((), ())), preferred_element_type=F32)


PEER_TT = 512
PEER_ROWS = 8
PEER_EB = PEER_ROWS * PEER_NKEYS
PEER_CHUNK = 256
PEER_NTOP = PEER_TOPK + 1
_CANDS = tuple((a, b) for a in range(PEER_NTOP) for b in range(PEER_NTOP)
               if (a + 1) * (b + 1) <= PEER_NTOP)


def _sorting_network(n):
    pairs = []
    p = 1
    while p < n:
        k = p
        while k >= 1:
            for j in range(k % p, n - k, 2 * k):
                for i in range(min(k, n - j - k)):
                    if (i + j) // (2 * p) == (i + j + k) // (2 * p):
                        pairs.append((i + j, i + j + k))
            k //= 2
        p *= 2
    return tuple(pairs)


def _top_values(x, n):
    n_tiles = x.shape[0] // SUBLANES
    rows = [x[r * SUBLANES:(r + 1) * SUBLANES] for r in range(n_tiles)]
    for a, b in _sorting_network(n_tiles):
        rows[a], rows[b] = jnp.maximum(rows[a], rows[b]), jnp.minimum(rows[a], rows[b])
    out = []
    for r in range(n):
        m = jnp.max(rows[0], axis=0, keepdims=True)
        out.append(m)
        taken = rows[0] == m
        for q in range(min(n_tiles, n - 1 - r)):
            below = rows[q + 1] if q + 1 < n_tiles else NEG_INF
            rows[q] = jnp.where(taken, below, rows[q])
    return out


def _peer_kernel(h_ref, g_ref, wqt_ref, keys_ref, u_ref, vt_ref, vprev_ref, o_ref,
                 xn_ref, s_ref, e1_ref, se2_ref, top_ref, tau_ref, acc_ref, ht_ref, act_ref, *, tt):
    eb = pl.program_id(1)
    nlb = tt // LANES
    n_ck = PEER_EB // PEER_CHUNK
    chunk = lambda ck: slice(ck * PEER_CHUNK, (ck + 1) * PEER_CHUNK)

    @pl.when(eb == 0)
    def _prepare():
        xn = _rms(h_ref[...], g_ref[...])
        xh, xl = _split(xn)
        xn_ref[...] = xh
        qt = (_dot_nt(wqt_ref[0], xh) + _dot_nt(wqt_ref[0], xl) + _dot_nt(wqt_ref[1], xh))
        for hc in range(2 * PEER_HEADS):
            qh, ql = _split(qt[hc * PEER_NKEYS:(hc + 1) * PEER_NKEYS, :])
            s_ref[hc] = (jnp.dot(keys_ref[0, hc], qh, preferred_element_type=F32)
                         + jnp.dot(keys_ref[0, hc], ql, preferred_element_type=F32)
                         + jnp.dot(keys_ref[1, hc], qh, preferred_element_type=F32))

        def per_lane_block(lb, carry):
            lanes = pl.ds(pl.multiple_of(lb * LANES, LANES), LANES)
            for hc in range(2 * PEER_HEADS):
                h, c = divmod(hc, 2)
                work = s_ref[hc, :, lanes]
                ex = jnp.exp(work - jnp.max(work, axis=0, keepdims=True))
                if c == 0:
                    e1_ref[h, :, lanes] = ex
                else:
                    for rb in range(PEER_NKEYS // SUBLANES):
                        rows = slice(rb * SUBLANES, (rb + 1) * SUBLANES)
                        se2_ref[lb, rb, 2 * h] = work[rows]
                        se2_ref[lb, rb, 2 * h + 1] = ex[rows]
                for r, m in enumerate(_top_values(work, PEER_NTOP)):
                    top_ref[c, r, pl.ds(h, 1), lanes] = m
            v1 = [top_ref[0, r, :, lanes] for r in range(PEER_NTOP)]
            v2 = [top_ref[1, r, :, lanes] for r in range(PEER_NTOP)]
            cands = [v1[a] + v2[b] for (a, b) in _CANDS]
            cv = []
            for r in range(PEER_TOPK + 1):
                m = functools.reduce(jnp.maximum, cands)
                cv.append(m)
                if r < PEER_TOPK:
                    cands = [jnp.where(x == m, NEG_INF, x) for x in cands]
            z = functools.reduce(lambda a, b: a + b, [jnp.exp(x - cv[0]) for x in cv[:PEER_TOPK]])
            tau_ref[0, :, lanes] = 0.5 * (cv[PEER_TOPK - 1] + cv[PEER_TOPK])
            tau_ref[1, :, lanes] = 1.0 / z
            return carry

        lax.fori_loop(0, nlb, per_lane_block, 0)
        acc_ref[...] = jnp.zeros_like(acc_ref)
        act_ref[PEER_EB // 2:, :] = jnp.zeros((PEER_EB // 2, tt), BF16)

    irows =pl.ds(pl.multiple_of(eb * PEER_ROWS, PEER_ROWS), PEER_ROWS)
    shape = (SUBLANES, LANES)

    def gate_rows(il, lb):
        lanes = slice(lb * LANES, (lb + 1) * LANES)
        tau = tau_ref[0, :, lanes]
        rz = tau_ref[1, :, lanes]
        theta, e1 = [], []
        for h in range(PEER_HEADS):
            theta.append(jnp.broadcast_to(tau[h:h + 1] - s_ref[2 * h, irows, lanes][il:il + 1], shape))
            e1.append(jnp.broadcast_to(e1_ref[h, irows, lanes][il:il + 1] * rz[h:h + 1], shape))
        for jb in range(PEER_NKEYS // BF16_ROWS):
            pieces = []
            for half in range(2):
                rb = 2 * jb + half
                j0 = rb * SUBLANES
                terms = [jnp.where(se2_ref[lb, rb, 2 * h] >= theta[h],
                                   e1[h] * se2_ref[lb, rb, 2 * h + 1], 0.0) for h in range(PEER_HEADS)]
                while len(terms) > 1:
                    terms = [terms[i] + terms[i + 1] for i in range(0, len(terms), 2)]
                r0 = il * PEER_NKEYS + j0
                hv = ht_ref[r0:r0 + SUBLANES, lanes]
                pieces.append(0.5 * hv * (1.0 + lax.erf(hv * (2.0 ** -0.5))) * terms[0])
            a0 = il * PEER_NKEYS + jb * BF16_ROWS
            act_ref[a0:a0 + BF16_ROWS, lanes] = jnp.concatenate(pieces, axis=0).astype(BF16)

    lo, hi = slice(0, PEER_EB // 2), slice(PEER_EB // 2, PEER_EB)

    def hval(ck):
        ht_ref[chunk(ck), :] = _dot_nt(u_ref[chunk(ck), :], xn_ref[...])

    def gates(cks):
        for ck in cks:
            for il in range(ck * PEER_CHUNK // PEER_NKEYS, (ck + 1) * PEER_CHUNK // PEER_NKEYS):
                for lb in range(nlb):
                    gate_rows(il, lb)

    first, second = range(n_ck // 2), range(n_ck // 2, n_ck)
    for ck in first:
        hval(ck)
    acc_ref[...] += jnp.dot(vprev_ref[...], act_ref[hi, :], preferred_element_type=F32)
    for ck in second:
        hval(ck)
    gates(first)
    acc_ref[...] += jnp.dot(vt_ref[:, lo], act_ref[lo, :], preferred_element_type=F32)
    gates(second)

    @pl.when(eb == pl.num_programs(1) - 1)
    def _finish():
        tail = jnp.dot(vt_ref[:, hi], act_ref[hi, :], preferred_element_type=F32)
        o_ref[...] = h_ref[...] + (acc_ref[...] + tail).T


def _peer(h, norm_g, wqt, keys, u, vt):
    t_total = h.shape[0]
    tt = min(PEER_TT, t_total)
    assert t_total % tt == 0 and tt % LANES == 0
    n_eb = PEER_EXPERTS // PEER_EB
    n_ck = PEER_EB // PEER_CHUNK
    kern = functools.partial(_peer_kernel, tt=tt)
    return pl.pallas_call(
        kern,
        grid=(t_total // tt, n_eb),
        in_specs=[
            pl.BlockSpec((tt, D_MODEL), lambda t, e: (t, 0)),
            pl.BlockSpec((1, D_MODEL), lambda t, e: (0, 0)),
            pl.BlockSpec((2, PEER_HEADS * PEER_DKEY, D_MODEL), lambda t, e: (0, 0, 0)),
            pl.BlockSpec((2, 2 * PEER_HEADS, PEER_NKEYS, PEER_DKEY // 2), lambda t, e: (0, 0, 0, 0)),
            pl.BlockSpec((PEER_EB, D_MODEL), lambda t, e: (e, 0)),
            pl.BlockSpec((D_MODEL, PEER_EB), lambda t, e: (0, e)),
            pl.BlockSpec((D_MODEL, PEER_EB // 2), lambda t, e: (0, jnp.maximum(2 * e - 1, 0))),
        ],
        out_specs=pl.BlockSpec((tt, D_MODEL), lambda t, e: (t, 0)),
        out_shape=jax.ShapeDtypeStruct((t_total, D_MODEL), F32),
        scratch_shapes=[
            pltpu.VMEM((tt, D_MODEL), BF16),
            pltpu.VMEM((2 * PEER_HEADS, PEER_NKEYS, tt), F32),
            pltpu.VMEM((PEER_HEADS, PEER_NKEYS, tt), F32),
            pltpu.VMEM((tt // LANES, PEER_NKEYS // SUBLANES, 2 * PEER_HEADS, SUBLANES, LANES), F32),
            pltpu.VMEM((2, PEER_NTOP, PEER_HEADS, tt), F32),
            pltpu.VMEM((2, PEER_HEADS, tt), F32),
            pltpu.VMEM((D_MODEL, tt), F32),
            pltpu.VMEM((PEER_EB, tt), F32),
            pltpu.VMEM((PEER_EB, tt), BF16),
        ],
        compiler_params=_cparams("parallel", "arbitrary"),
        name="peer",
    )(h, norm_g, wqt, keys, u, vt, vt)


TOKEN_TILE = 256


def _inproj_kernel(x_ref, g_ref, wr_ref, ww_ref, pr_ref, pw_ref):
    xn = _rms(x_ref[...], g_ref[...]).astype(BF16)
    pr_ref[...] = jnp.dot(xn, wr_ref[...], preferred_element_type=F32)
    pw_ref[...] = jnp.dot(xn, ww_ref[...], preferred_element_type=F32)


def _inproj(x, norm_g, w_ret, w_rw):
    t_total = x.shape[0]
    tm = min(TOKEN_TILE, t_total)
    assert t_total % tm == 0
    const = lambda shape: pl.BlockSpec(shape, lambda t: (0,) * len(shape))
    return pl.pallas_call(
        _inproj_kernel,
        grid=(t_total // tm,),
        in_specs=[pl.BlockSpec((tm, D_MODEL), lambda t: (t, 0)), const((1, D_MODEL)),
                  const((D_MODEL, RET_PROJ)), const((D_MODEL, RWKV_PROJ))],
        out_specs=[pl.BlockSpec((tm, RET_PROJ), lambda t: (t, 0)),
                   pl.BlockSpec((tm, RWKV_PROJ), lambda t: (t, 0))],
        out_shape=[jax.ShapeDtypeStruct((t_total, RET_PROJ), F32),
                   jax.ShapeDtypeStruct((t_total, RWKV_PROJ), F32)],
        compiler_params=_cparams("parallel"),
        name="inproj",
    )(x, norm_g, w_ret, w_rw)


def _rope_partner(x):
    lane = lax.broadcasted_iota(jnp.int32, x.shape, 1)
    up = pltpu.roll(x, x.shape[1] - RET_DK // 2, 1)
    down = pltpu.roll(x, RET_DK // 2, 1)
    return jnp.where(lane % RET_DK < RET_DK // 2, up, down)


def _retention_kernel(p_ref, s0_ref, cos_ref, sin_ref, dmat_ref, xz_ref, ones_ref, o_ref, so_ref, s_ref,
                      *, nb, g_chunk):
    ci = pl.program_id(1)
    qk = RET_HEADS * RET_DK
    n_pairs = RET_HEADS // 2
    row_head = lax.broadcasted_iota(jnp.int32, (LANES, LANES), 0) // RET_DK
    lane_head = lax.broadcasted_iota(jnp.int32, (LANES, LANES), 1) // RET_DV
    same_head = row_head == lane_head

    @pl.when(ci == 0)
    def _init():
        s_ref[...] = jnp.zeros_like(s_ref)
        for p in range(n_pairs):
            s_ref[:, p, :RET_DK, :RET_DV] = s0_ref[:, 2 * p]
            s_ref[:, p, RET_DK:, RET_DV:] = s0_ref[:, 2 * p + 1]

    cos = cos_ref[...]
    sin = sin_ref[...]

    def per_seq(n, carry):
        x = p_ref[n]
        q = x[:, :qk]
        k = x[:, qk:2 * qk]
        q = q * cos + _rope_partner(q) * sin
        k = (k * cos + _rope_partner(k) * sin) * (RET_DK ** -0.5)
        lane_is_hi = lax.broadcasted_iota(jnp.int32, (q.shape[0], LANES), 1) >= RET_DK
        for p in range(n_pairs):
            ls = slice(p * LANES, (p + 1) * LANES)
            qp, kp, vp = q[:, ls], k[:, ls], x[:, 2 * qk + p * LANES:2 * qk + (p + 1) * LANES]
            kb, vb = kp.astype(BF16), vp.astype(BF16)
            s_p = s_ref[n, p]
            o = jnp.dot(qp.astype(BF16), s_p.astype(BF16), preferred_element_type=F32) * xz_ref[0, p]
            for hh in range(2):
                mine = lane_is_hi if hh else jnp.logical_not(lane_is_hi)
                inner = _dot_nt(jnp.where(mine, qp, 0.0).astype(BF16), kb) * dmat_ref[2 * p + hh]
                o = o + jnp.dot(inner.astype(BF16), jnp.where(mine, vp, 0.0).astype(BF16),
                                preferred_element_type=F32)
            o_ref[n, :, ls] = o
            decay = jnp.where(row_head == 0, g_chunk[2 * p], g_chunk[2 * p + 1])
            grown = _dot_tn((kp * xz_ref[1, p]).astype(BF16), vb)
            s_ref[n, p] = s_p * decay + jnp.where(same_head, grown, 0.0)
        o = o_ref[n]
        g = x[:, 2 * qk + RET_W:]
        oc = o - _head_sums(o, ones_ref[...]) * (1.0 / RET_DV)
        var = _head_sums(oc * oc, ones_ref[...]) * (1.0 / RET_DV)
        o_ref[n] = oc * lax.rsqrt(var + GN_EPS) * (g * jax.nn.sigmoid(g))
        return carry

    lax.fori_loop(0, nb, per_seq, 0)

    @pl.when(ci == pl.num_programs(1) - 1)
    def _done():
        for p in range(n_pairs):
            so_ref[:, 2 * p] = s_ref[:, p, :RET_DK, :RET_DV]
            so_ref[:, 2 * p + 1] = s_ref[:, p, RET_DK:, RET_DV:]


def _retention(proj_ret, s0, pos, head_ones):
    assert RET_W == RWKV_W and RET_DV == RWKV_N
    b, l, _ = proj_ret.shape
    c = RET_CHUNK if l % RET_CHUNK == 0 else l
    nb = 1 if c == RET_CHUNK else 16
    assert b % nb == 0
    half = RET_DK // 2
    inv = ROPE_BASE ** (-jnp.arange(half, dtype=F32) / half)
    ang = pos.astype(F32)[:, None] * inv[None, :]
    cos = jnp.tile(jnp.cos(ang), (1, 2 * RET_HEADS))
    sin = jnp.tile(jnp.concatenate([-jnp.sin(ang), jnp.sin(ang)], axis=1), (1, RET_HEADS))
    lg = jnp.log(1.0 - 2.0 ** (-5.0 - jnp.arange(RET_HEADS, dtype=F32)))
    idx = jnp.arange(c, dtype=F32)
    diff = idx[:, None] - idx[None, :]
    dmat = jnp.where(diff[None] >= 0, jnp.exp(jnp.maximum(diff, 0.0)[None] * lg[:, None, None]), 0.0)
    xi = jnp.exp((idx + 1.0)[None, :] * lg[:, None])
    zeta = jnp.exp((c - 1.0 - idx)[None, :] * lg[:, None])
    pair_lanes = lambda a: jnp.repeat(a.reshape(RET_HEADS // 2, 2, c).transpose(0, 2, 1), RET_DV, axis=-1)
    xz = jnp.stack([pair_lanes(xi), pair_lanes(zeta)])
    g_chunk = tuple(math.exp(c * math.log(1.0 - 2.0 ** (-5.0 - h))) for h in range(RET_HEADS))
    kern = functools.partial(_retention_kernel, nb=nb, g_chunk=g_chunk)
    return pl.pallas_call(
        kern,
        grid=(b // nb, l // c),
        in_specs=[
            pl.BlockSpec((nb, c, RET_PROJ), lambda i, j: (i, j, 0)),
            pl.BlockSpec((nb, RET_HEADS, RET_DK, RET_DV), lambda i, j: (i, 0, 0, 0)),
            pl.BlockSpec((c, RET_HEADS * RET_DK), lambda i, j: (j, 0)),
            pl.BlockSpec((c, RET_HEADS * RET_DK), lambda i, j: (j, 0)),
            pl.BlockSpec((RET_HEADS, c, c), lambda i, j: (0, 0, 0)),
            pl.BlockSpec((2, RET_HEADS // 2, c, LANES), lambda i, j: (0, 0, 0, 0)),
            pl.BlockSpec((RET_W, RET_W), lambda i, j: (0, 0)),
        ],
        out_specs=[pl.BlockSpec((nb, c, RET_W), lambda i, j: (i, j, 0)),
                   pl.BlockSpec((nb, RET_HEADS, RET_DK, RET_DV), lambda i, j: (i, 0, 0, 0))],
        out_shape=[jax.ShapeDtypeStruct((b, l, RET_W), F32),
                   jax.ShapeDtypeStruct((b, RET_HEADS, RET_DK, RET_DV), F32)],
        scratch_shapes=[pltpu.VMEM((nb, RET_HEADS // 2, 2 * RET_DK, 2 * RET_DV), F32)],
        compiler_params=_cparams("parallel", "arbitrary"),
        name="retention",
    )(proj_ret, s0, cos, sin, dmat, xz, head_ones)


def _softplus(x):
    return jnp.maximum(x, 0.0) + jnp.log1p(jnp.exp(-jnp.abs(x)))


def _rwkv_prep_kernel(x_ref, sh_ref, mu_ref, w0_ref, w2_ref, a0_ref, a2_ref, g2_ref, kks_ref, ka_ref,
                      rk_ref, ones_ref, r_ref, w_ref, k_ref, kk_ref, cc_ref, v_ref, gate_ref, bonus_ref,
                      carry_ref, *, nb, tb):
    ti = pl.program_id(1)
    x = x_ref[...]
    first = jnp.where(ti == 0, sh_ref[...], carry_ref[...])
    prev = pltpu.roll(x.reshape(nb * tb, RWKV_PROJ), 1, 0).reshape(nb, tb, RWKV_PROJ)
    tpos = lax.broadcasted_iota(jnp.int32, x.shape, 1)
    prev = jnp.where(tpos == 0, first, prev)
    carry_ref[...] = x[:, tb - 1:tb, :]
    f = (x + (prev - x) * mu_ref[...]).reshape(nb * tb, RWKV_PROJ)
    o1, o2, o3 = RWKV_W, 2 * RWKV_W, 3 * RWKV_W
    r, k7, v7 = f[:, :o1], f[:, o1:o2], f[:, o2:o3]
    fw = f[:, o3:o3 + DECAY_LORA]
    fa = f[:, o3 + DECAY_LORA:o3 + DECAY_LORA + AAA_LORA]
    fg = f[:, o3 + DECAY_LORA + AAA_LORA:]
    w = -_softplus(-(w0_ref[...] + _dot3(jnp.tanh(fw), w2_ref[...]))) - 0.5
    a = jax.nn.sigmoid(a0_ref[...] + _dot3(fa, a2_ref[...]))
    gate = _dot3(jax.nn.sigmoid(fg), g2_ref[...])
    kk = k7 * kks_ref[...]
    kk = kk / jnp.maximum(jnp.sqrt(_head_sums(kk * kk, ones_ref[...])), 1e-12)
    k = k7 * (1.0 + (a - 1.0) * ka_ref[...])
    out3 = lambda y: y.reshape(nb, tb, RWKV_W)
    r_ref[...] = out3(r)
    w_ref[...] = out3(jnp.exp(-jnp.exp(w)))
    k_ref[...] = out3(k)
    kk_ref[...] = out3(kk)
    cc_ref[...] = out3(kk * a)
    v_ref[...] = out3(v7)
    gate_ref[...] = out3(gate)
    bonus_ref[...] = out3(_head_sums(r * k * rk_ref[...], ones_ref[...]) * v7)


def _rwkv_prep(proj_rw, shift, wt):
    b, l, _ = proj_rw.shape
    tb = min(TOKEN_TILE, l)
    nb = max(1, 128 // tb)
    assert l % tb == 0 and b % nb == 0 and tb % SUBLANES == 0
    const = lambda shape: pl.BlockSpec(shape, lambda i, j: (0,) * len(shape))
    blk = pl.BlockSpec((nb, tb, RWKV_W), lambda i, j: (i, j, 0))
    kern = functools.partial(_rwkv_prep_kernel, nb=nb, tb=tb)
    return pl.pallas_call(
        kern,
        grid=(b // nb, l // tb),
        in_specs=[
            pl.BlockSpec((nb, tb, RWKV_PROJ), lambda i, j: (i, j, 0)),
            pl.BlockSpec((nb, 1, RWKV_PROJ), lambda i, j: (i, 0, 0)),
            const((1, RWKV_PROJ)), const((1, RWKV_W)), const((DECAY_LORA, RWKV_W)),
            const((1, RWKV_W)), const((AAA_LORA, RWKV_W)), const((GATE_LORA, RWKV_W)),
            const((1, RWKV_W)), const((1, RWKV_W)), const((1, RWKV_W)), const((RWKV_W, RWKV_W)),
        ],
        out_specs=[blk] * 8,
        out_shape=[jax.ShapeDtypeStruct((b, l, RWKV_W), F32)] * 8,
        scratch_shapes=[pltpu.VMEM((nb, 1, RWKV_PROJ), F32)],
        compiler_params=_cparams("parallel", "arbitrary"),
        name="rwkv_prep",
    )(proj_rw, shift.reshape(b, 1, RWKV_PROJ), wt['mu'], wt['w0'], wt['w2'], wt['a0'], wt['a2'],
      wt['g2'], wt['kk'], wt['ka'], wt['rk'], wt['head_ones'])


def _wkv_kernel(kk_ref, w_ref, cc_ref, k_ref, r_ref, v_ref, s0_ref, y_ref, so_ref, s_ref, *, tb, ni):
    ti = pl.program_id(1)

    @pl.when(ti == 0)
    def _init():
        s_ref[...] = s0_ref[0]

    def step(t, carry):
        kk, w, cc, k, r = kk_ref[0, t], w_ref[0, t], cc_ref[0, t], k_ref[0, t], r_ref[0, t]
        for ii in range(ni):
            s = s_ref[ii]
            sa = -jnp.sum(s * kk, axis=0, keepdims=True)
            sn = s * w + sa * cc + v_ref[0, t, ii:ii + 1, :] * k
            s_ref[ii] = sn
            y_ref[0, t, ii:ii + 1, :] = jnp.sum(sn * r, axis=0, keepdims=True)
        return carry

    lax.fori_loop(0, tb, step, 0)

    @pl.when(ti == pl.num_programs(1) - 1)
    def _done():
        so_ref[0] = s_ref[...]


def _wkv_scan(kk, w, cc, k, r, v, s0):
    g, l, ni, _ = v.shape
    tb = min(32, l)
    assert l % tb == 0
    jblk = pl.BlockSpec((1, tb, RWKV_N, LANES), lambda i, j: (i, j, 0, 0))
    iblk = pl.BlockSpec((1, tb, ni, LANES), lambda i, j: (i, j, 0, 0))
    sblk = pl.BlockSpec((1, ni, RWKV_N, LANES), lambda i, j: (i, 0, 0, 0))
    kern = functools.partial(_wkv_kernel, tb=tb, ni=ni)
    return pl.pallas_call(
        kern,
        grid=(g, l // tb),
        in_specs=[jblk] * 5 + [iblk, sblk],
        out_specs=[iblk, sblk],
        out_shape=[jax.ShapeDtypeStruct((g, l, ni, LANES), F32),
                   jax.ShapeDtypeStruct((g, ni, RWKV_N, LANES), F32)],
        scratch_shapes=[pltpu.VMEM((ni, RWKV_N, LANES), F32)],
        compiler_params=_cparams("parallel", "arbitrary"),
        name="wkv_scan",
    )(kk, w, cc, k, r, v, s0)


def _outproj_kernel(x_ref, oret_ref, y_ref, bonus_ref, gate_ref, lng_ref, lnb_ref, ones_ref,
                    wtop_ref, wbot_ref, o_ref):
    y = y_ref[...]
    yc = y - _head_sums(y, ones_ref[...]) * (1.0 / RWKV_N)
    var = _head_sums(yc * yc, ones_ref[...]) * (1.0 / RWKV_N)
    o = yc * lax.rsqrt(var + RWKV_LN_EPS) * lng_ref[...] + lnb_ref[...]
    o = (o + bonus_ref[...]) * gate_ref[...]
    o_ref[...] = (x_ref[...]
                  + jnp.dot(oret_ref[...].astype(BF16), wtop_ref[...], preferred_element_type=F32)
                  + jnp.dot(o.astype(BF16), wbot_ref[...], preferred_element_type=F32))


def _outproj(x, o_ret, y, bonus, gate, wt):
    t_total = x.shape[0]
    tm = min(TOKEN_TILE, t_total)
    const = lambda shape: pl.BlockSpec(shape, lambda t: (0,) * len(shape))
    half = pl.BlockSpec((tm, RWKV_W), lambda t: (t, 0))
    full = pl.BlockSpec((tm, D_MODEL), lambda t: (t, 0))
    return pl.pallas_call(
        _outproj_kernel,
        grid=(t_total // tm,),
        in_specs=[full, half, half, half, half, const((1, RWKV_W)), const((1, RWKV_W)),
                  const((RWKV_W, RWKV_W)), const((RET_W, D_MODEL)), const((RWKV_W, D_MODEL))],
        out_specs=full,
        out_shape=jax.ShapeDtypeStruct((t_total, D_MODEL), F32),
        compiler_params=_cparams("parallel"),
        name="outproj",
    )(x, o_ret, y, bonus, gate, wt['ln_g'], wt['ln_b'], wt['head_ones'], wt['w_out_top'], wt['w_out_bot'])


def _ple_kernel(h_ref, p_ref, gn_ref, wg_ref, wp_ref, fn_ref, o_ref):
    h = h_ref[...]
    gate = jax.nn.sigmoid(jnp.dot(_rms(h, gn_ref[...]).astype(BF16), wg_ref[...], preferred_element_type=F32))
    h = h + gate * jnp.dot(p_ref[...].astype(BF16), wp_ref[...], preferred_element_type=F32)
    o_ref[...] = _rms(h, fn_ref[...])


def _ple(h, p, wt):
    t_total = h.shape[0]
    tm = min(TOKEN_TILE, t_total)
    const = lambda shape: pl.BlockSpec(shape, lambda t: (0,) * len(shape))
    full = pl.BlockSpec((tm, D_MODEL), lambda t: (t, 0))
    return pl.pallas_call(
        _ple_kernel,
        grid=(t_total // tm,),
        in_specs=[full, pl.BlockSpec((tm, PLE_DIM), lambda t: (t, 0)), const((1, D_MODEL)),
                  const((D_MODEL, D_MODEL)), const((PLE_DIM, D_MODEL)), const((1, D_MODEL))],
        out_specs=full,
        out_shape=jax.ShapeDtypeStruct((t_total, D_MODEL), F32),
        compiler_params=_cparams("parallel"),
        name="ple",
    )(h, p, wt['ple_norm'], wt['ple_gate'], wt['ple_proj'], wt['norm_final'])


def _prep_weights(p):
    row = lambda a: a.reshape(1, -1)
    lane_head = jnp.arange(RWKV_W) // RWKV_N
    return {
        'norm_mix': row(p['norm_mix'][0]),
        'w_ret': p['w_in'][0][:, :RET_PROJ].astype(BF16),
        'w_rw': p['w_in'][0][:, RET_PROJ:].astype(BF16),
        'mu': row(p['rw_mu'][0]), 'w0': row(p['rw_w0'][0]), 'w2': p['rw_w2'][0],
        'a0': row(p['rw_a0'][0]), 'a2': p['rw_a2'][0], 'g2': p['rw_g2'][0],
        'kk': row(p['rw_kk'][0]), 'ka': row(p['rw_ka'][0]), 'rk': row(p['rw_rk'][0]),
        'ln_g': row(p['rw_ln_g'][0]), 'ln_b': row(p['rw_ln_b'][0]),
        'head_ones': (lane_head[:, None] == lane_head[None, :]).astype(BF16),
        'w_out_top': p['w_out'][0][:RET_W].astype(BF16),
        'w_out_bot': p['w_out'][0][RET_W:].astype(BF16),
        'norm_ffn': row(p['norm_ffn'][0]),
        'wqt': jnp.stack(_split(p['peer_wq'][0].T)),
        'keys': jnp.stack(_split(p['peer_keys'][0].reshape(2 * PEER_HEADS, PEER_NKEYS, PEER_DKEY // 2))),
        'u': p['peer_u'][0].astype(BF16),
        'vt': p['peer_v'][0].astype(BF16).T,
        'ple_norm': row(p['ple_norm'][0]),
        'ple_gate': p['ple_gate'][0].astype(BF16),
        'ple_proj': p['ple_proj'][0].astype(BF16),
        'norm_final': row(p['norm_final']),
    }


def _chains_to_lanes(a, split_values):
    b, l, _ = a.shape
    a = a.reshape(b, l, RWKV_HEADS, RWKV_N)
    if b * RWKV_HEADS * 2 == LANES:
        if split_values:
            a = a.reshape(b, l, RWKV_HEADS, 2, RWKV_N // 2).transpose(1, 4, 3, 0, 2)
            return a.reshape(1, l, RWKV_N // 2, LANES)
        a = a.transpose(1, 3, 0, 2).reshape(l, RWKV_N, 1, LANES // 2)
        return jnp.broadcast_to(a, (l, RWKV_N, 2, LANES // 2)).reshape(1, l, RWKV_N, LANES)
    assert b == LANES
    return a.transpose(2, 1, 3, 0)


def _values_from_lanes(y, b):
    g, l, ni, _ = y.shape
    if g == 1:
        y = y.reshape(l, ni, 2, b, RWKV_HEADS).transpose(3, 0, 4, 2, 1)
    else:
        y = y.transpose(3, 1, 0, 2)
    return y.reshape(b, l, RWKV_W)


def _state_to_lanes(s):
    b = s.shape[0]
    if b * RWKV_HEADS * 2 == LANES:
        s = s.reshape(b, RWKV_HEADS, 2, RWKV_N // 2, RWKV_N).transpose(3, 4, 2, 0, 1)
        return s.reshape(1, RWKV_N // 2, RWKV_N, LANES)
    return s.transpose(1, 2, 3, 0)


def _state_from_lanes(s, b):
    g, ni, _, _ = s.shape
    if g == 1:
        s = s.reshape(ni, RWKV_N, 2, b, RWKV_HEADS).transpose(3, 4, 2, 0, 1)
        return s.reshape(b, RWKV_HEADS, RWKV_N, RWKV_N)
    return s.transpose(3, 0, 1, 2)


def _layer(x, p_l, pos, s_ret, s_wkv, s_shift, wt):
    b, l, _ = x.shape
    x2 = x.reshape(b * l, D_MODEL)
    proj_ret, proj_rw = _inproj(x2, wt['norm_mix'], wt['w_ret'], wt['w_rw'])
    o_ret, s_ret_new = _retention(proj_ret.reshape(b, l, RET_PROJ), s_ret, pos, wt['head_ones'])
    proj_rw = proj_rw.reshape(b, l, RWKV_PROJ)
    new_shift = proj_rw[:, -1]
    r, w, k, kk, cc, v, gate, bonus = _rwkv_prep(proj_rw, s_shift, wt)
    y, s_lanes = _wkv_scan(*(_chains_to_lanes(a, False) for a in (kk, w, cc, k, r)),
                           _chains_to_lanes(v, True), _state_to_lanes(s_wkv))
    y = _values_from_lanes(y, b)
    s_wkv_new = _state_from_lanes(s_lanes, b)
    flat = lambda a: a.reshape(b * l, -1)
    h = _outproj(x2, flat(o_ret), flat(y), flat(bonus), flat(gate), wt)
    h = _peer(h, wt['norm_ffn'], wt['wqt'], wt['keys'], wt['u'], wt['vt'])
    y_out = _ple(h, p_l.reshape(b * l, PLE_DIM), wt)
    return y_out.reshape(b, l, D_MODEL), s_ret_new, s_wkv_new, new_shift


def kernel(x_prompt, x_sample, state_ret, state_wkv, state_shift, p_prompt, p_sample, norm_mix, w_in, rw_mu, rw_w0, rw_w2, rw_a0, rw_a2, rw_g2, rw_kk, rw_ka, rw_rk, rw_ln_g, rw_ln_b, w_out, norm_ffn, peer_wq, peer_keys, peer_u, peer_v, ple_norm, ple_gate, ple_proj, norm_final):
    assert norm_mix.shape[0] == 1, "single-layer trunk"
    wt = _prep_weights(dict(
        norm_mix=norm_mix, w_in=w_in, rw_mu=rw_mu, rw_w0=rw_w0, rw_w2=rw_w2, rw_a0=rw_a0, rw_a2=rw_a2,
        rw_g2=rw_g2, rw_kk=rw_kk, rw_ka=rw_ka, rw_rk=rw_rk, rw_ln_g=rw_ln_g, rw_ln_b=rw_ln_b, w_out=w_out,
        norm_ffn=norm_ffn, peer_wq=peer_wq, peer_keys=peer_keys, peer_u=peer_u, peer_v=peer_v,
        ple_norm=ple_norm, ple_gate=ple_gate, ple_proj=ple_proj, norm_final=norm_final))
    bp, lp, _ = x_prompt.shape
    bs, ls, _ = x_sample.shape
    pos_p = jnp.arange(lp, dtype=jnp.int32)
    pos_s = PAST_LEN + jnp.arange(ls, dtype=jnp.int32)
    zero_ret = jnp.zeros((bp, RET_HEADS, RET_DK, RET_DV), F32)
    zero_wkv = jnp.zeros((bp, RWKV_HEADS, RWKV_N, RWKV_N), F32)
    zero_shift = jnp.zeros((bp, RWKV_PROJ), F32)
    yp, rp, wp, sp = _layer(x_prompt, p_prompt[0], pos_p, zero_ret, zero_wkv, zero_shift, wt)
    ys, rs, ws, ss = _layer(x_sample, p_sample[0], pos_s, state_ret[0], state_wkv[0], state_shift[0], wt)
    return (yp, ys, rp[None], wp[None], sp[None], rs[None], ws[None], ss[None])
```

```python
import functools
import math

import jax
import jax.numpy as jnp
from jax import lax
from jax.experimental import pallas as pl
from jax.experimental.pallas import tpu as pltpu

F32 = jnp.float32
BF16 = jnp.bfloat16

D_MODEL = 1024
PAST_LEN = 16384
RET_HEADS = 8
RET_DK = 64
RET_DV = 64
RET_W = RET_HEADS * RET_DV
RWKV_HEADS = 8
RWKV_N = 64
RWKV_W = RWKV_HEADS * RWKV_N
DECAY_LORA = 64
AAA_LORA = 64
GATE_LORA = 128
RET_PROJ = 2 * RET_HEADS * RET_DK + 2 * RET_W
RWKV_PROJ = 3 * RWKV_W + DECAY_LORA + AAA_LORA + GATE_LORA
RET_CHUNK = 128
ROPE_BASE = 10000.0
PEER_HEADS = 8
PEER_NKEYS = 128
PEER_EXPERTS = PEER_NKEYS * PEER_NKEYS
PEER_DKEY = 256
PEER_TOPK = 16
PLE_DIM = 256
RMS_EPS = 1e-6
GN_EPS = 1e-5
RWKV_LN_EPS = 64e-5

LANES = 128
SUBLANES = 8
BF16_ROWS = 16
VMEM_LIMIT = 56 * 1024 * 1024

NEG_INF = float("-inf")


def _cparams(*sem):
    return pltpu.CompilerParams(dimension_semantics=sem, vmem_limit_bytes=VMEM_LIMIT)


def _rms(x, g):
    return x * lax.rsqrt(jnp.mean(x * x, axis=-1, keepdims=True) + RMS_EPS) * g


def _split(x):
    hi = x.astype(BF16)
    return hi, (x - hi.astype(F32)).astype(BF16)


def _dot3(a, b):
    ah, al = _split(a)
    bh, bl = _split(b)
    dot = functools.partial(jnp.dot, preferred_element_type=F32)
    return dot(ah, bh) + dot(ah, bl) + dot(al, bh)


def _head_sums(x, head_ones):
    xh, xl = _split(x)
    dot = functools.partial(jnp.dot, preferred_element_type=F32)
    return dot(xh, head_ones) + dot(xl, head_ones)


def _dot_nt(a, b):
    return lax.dot_general(a, b, (((1,), (1,)), ((), ())), preferred_element_type=F32)


def _dot_tn(a, b):
    return lax.dot_general(a, b, (((0,), (0,)), ((), ())), preferred_element_type=F32)


PEER_TT = 512
PEER_ROWS = 8
PEER_EB = PEER_ROWS * PEER_NKEYS
PEER_CHUNK = 256
PEER_NTOP = PEER_TOPK + 1
_CANDS = tuple((a, b) for a in range(PEER_NTOP) for b in range(PEER_NTOP)
               if (a + 1) * (b + 1) <= PEER_NTOP)


def _sorting_network(n):
    pairs = []
    p = 1
    while p < n:
        k = p
        while k >= 1:
            for j in range(k % p, n - k, 2 * k):
                for i in range(min(k, n - j - k)):
                    if (i + j) // (2 * p) == (i + j + k) // (2 * p):
                        pairs.append((i + j, i + j + k))
            k //= 2
        p *= 2
    return tuple(pairs)


def _top_values(x, n):
    n_tiles = x.shape[0] // SUBLANES
    rows = [x[r * SUBLANES:(r + 1) * SUBLANES] for r in range(n_tiles)]
    for a, b in _sorting_network(n_tiles):
        rows[a], rows[b] = jnp.maximum(rows[a], rows[b]), jnp.minimum(rows[a], rows[b])
    out = []
    for r in range(n):
        m = jnp.max(rows[0], axis=0, keepdims=True)
        out.append(m)
        taken = rows[0] == m
        for q in range(min(n_tiles, n - 1 - r)):
            below = rows[q + 1] if q + 1 < n_tiles else NEG_INF
            rows[q] = jnp.where(taken, below, rows[q])
    return out


def _peer_kernel(h_ref, g_ref, wqt_ref, keys_ref, u_ref, vt_ref, vprev_ref, o_ref,
                 xn_ref, s_ref, e1_ref, se2_ref, top_ref, tau_ref, acc_ref, ht_ref, act_ref, *, tt):
    eb = pl.program_id(1)
    nlb = tt // LANES
    n_ck = PEER_EB // PEER_CHUNK
    chunk = lambda ck: slice(ck * PEER_CHUNK, (ck + 1) * PEER_CHUNK)

    @pl.when(eb == 0)
    def _prepare():
        xn = _rms(h_ref[...], g_ref[...])
        xh, xl = _split(xn)
        xn_ref[...] = xh
        qt = (_dot_nt(wqt_ref[0], xh) + _dot_nt(wqt_ref[0], xl) + _dot_nt(wqt_ref[1], xh))
        for hc in range(2 * PEER_HEADS):
            qh, ql = _split(qt[hc * PEER_NKEYS:(hc + 1) * PEER_NKEYS, :])
            s_ref[hc] = (jnp.dot(keys_ref[0, hc], qh, preferred_element_type=F32)
                         + jnp.dot(keys_ref[0, hc], ql, preferred_element_type=F32)
                         + jnp.dot(keys_ref[1, hc], qh, preferred_element_type=F32))

        def per_lane_block(lb, carry):
            lanes = pl.ds(pl.multiple_of(lb * LANES, LANES), LANES)
            for hc in range(2 * PEER_HEADS):
                h, c = divmod(hc, 2)
                work = s_ref[hc, :, lanes]
                ex = jnp.exp(work - jnp.max(work, axis=0, keepdims=True))
                if c == 0:
                    e1_ref[h, :, lanes] = ex
                else:
                    for rb in range(PEER_NKEYS // SUBLANES):
                        rows = slice(rb * SUBLANES, (rb + 1) * SUBLANES)
                        se2_ref[lb, rb, 2 * h] = work[rows]
                        se2_ref[lb, rb, 2 * h + 1] = ex[rows]
                for r, m in enumerate(_top_values(work, PEER_NTOP)):
                    top_ref[c, r, pl.ds(h, 1), lanes] = m
            v1 = [top_ref[0, r, :, lanes] for r in range(PEER_NTOP)]
            v2 = [top_ref[1, r, :, lanes] for r in range(PEER_NTOP)]
            cands = [v1[a] + v2[b] for (a, b) in _CANDS]
            cv = []
            for r in range(PEER_TOPK + 1):
                m = functools.reduce(jnp.maximum, cands)
                cv.append(m)
                if r < PEER_TOPK:
                    cands = [jnp.where(x == m, NEG_INF, x) for x in cands]
            z = functools.reduce(lambda a, b: a + b, [jnp.exp(x - cv[0]) for x in cv[:PEER_TOPK]])
            tau_ref[0, :, lanes] = 0.5 * (cv[PEER_TOPK - 1] + cv[PEER_TOPK])
            tau_ref[1, :, lanes] = 1.0 / z
            return carry

        lax.fori_loop(0, nlb, per_lane_block, 0)
        acc_ref[...] = jnp.zeros_like(acc_ref)
        act_ref[PEER_EB // 2:, :] = jnp.zeros((PEER_EB // 2, tt), BF16)

    irows =pl.ds(pl.multiple_of(eb * PEER_ROWS, PEER_ROWS), PEER_ROWS)
    shape = (SUBLANES, LANES)

    def gate_rows(il, lb):
        lanes = slice(lb * LANES, (lb + 1) * LANES)
        tau = tau_ref[0, :, lanes]
        rz = tau_ref[1, :, lanes]
        theta, e1 = [], []
        for h in range(PEER_HEADS):
            theta.append(jnp.broadcast_to(tau[h:h + 1] - s_ref[2 * h, irows, lanes][il:il + 1], shape))
            e1.append(jnp.broadcast_to(e1_ref[h, irows, lanes][il:il + 1] * rz[h:h + 1], shape))
        for jb in range(PEER_NKEYS // BF16_ROWS):
            pieces = []
            for half in range(2):
                rb = 2 * jb + half
                j0 = rb * SUBLANES
                terms = [jnp.where(se2_ref[lb, rb, 2 * h] >= theta[h],
                                   e1[h] * se2_ref[lb, rb, 2 * h + 1], 0.0) for h in range(PEER_HEADS)]
                while len(terms) > 1:
                    terms = [terms[i] + terms[i + 1] for i in range(0, len(terms), 2)]
                r0 = il * PEER_NKEYS + j0
                hv = ht_ref[r0:r0 + SUBLANES, lanes]
                pieces.append(0.5 * hv * (1.0 + lax.erf(hv * (2.0 ** -0.5))) * terms[0])
            a0 = il * PEER_NKEYS + jb * BF16_ROWS
            act_ref[a0:a0 + BF16_ROWS, lanes] = jnp.concatenate(pieces, axis=0).astype(BF16)

    lo, hi = slice(0, PEER_EB // 2), slice(PEER_EB // 2, PEER_EB)

    def hval(ck):
        ht_ref[chunk(ck), :] = _dot_nt(u_ref[chunk(ck), :], xn_ref[...])

    def gates(cks):
        for ck in cks:
            for il in range(ck * PEER_CHUNK // PEER_NKEYS, (ck + 1) * PEER_CHUNK // PEER_NKEYS):
                for lb in range(nlb):
                    gate_rows(il, lb)

    first, second = range(n_ck // 2), range(n_ck // 2, n_ck)
    for ck in first:
        hval(ck)
    acc_ref[...] += jnp.dot(vprev_ref[...], act_ref[hi, :], preferred_element_type=F32)
    for ck in second:
        hval(ck)
    gates(first)
    acc_ref[...] += jnp.dot(vt_ref[:, lo], act_ref[lo, :], preferred_element_type=F32)
    gates(second)

    @pl.when(eb == pl.num_programs(1) - 1)
    def _finish():
        tail = jnp.dot(vt_ref[:, hi], act_ref[hi, :], preferred_element_type=F32)
        o_ref[...] = h_ref[...] + (acc_ref[...] + tail).T


def _peer(h, norm_g, wqt, keys, u, vt):
    t_total = h.shape[0]
    tt = min(PEER_TT, t_total)
    assert t_total % tt == 0 and tt % LANES == 0
    n_eb = PEER_EXPERTS // PEER_EB
    n_ck = PEER_EB // PEER_CHUNK
    kern = functools.partial(_peer_kernel, tt=tt)
    return pl.pallas_call(
        kern,
        grid=(t_total // tt, n_eb),
        in_specs=[
            pl.BlockSpec((tt, D_MODEL), lambda t, e: (t, 0)),
            pl.BlockSpec((1, D_MODEL), lambda t, e: (0, 0)),
            pl.BlockSpec((2, PEER_HEADS * PEER_DKEY, D_MODEL), lambda t, e: (0, 0, 0)),
            pl.BlockSpec((2, 2 * PEER_HEADS, PEER_NKEYS, PEER_DKEY // 2), lambda t, e: (0, 0, 0, 0)),
            pl.BlockSpec((PEER_EB, D_MODEL), lambda t, e: (e, 0)),
            pl.BlockSpec((D_MODEL, PEER_EB), lambda t, e: (0, e)),
            pl.BlockSpec((D_MODEL, PEER_EB // 2), lambda t, e: (0, jnp.maximum(2 * e - 1, 0))),
        ],
        out_specs=pl.BlockSpec((tt, D_MODEL), lambda t, e: (t, 0)),
        out_shape=jax.ShapeDtypeStruct((t_total, D_MODEL), F32),
        scratch_shapes=[
            pltpu.VMEM((tt, D_MODEL), BF16),
            pltpu.VMEM((2 * PEER_HEADS, PEER_NKEYS, tt), F32),
            pltpu.VMEM((PEER_HEADS, PEER_NKEYS, tt), F32),
            pltpu.VMEM((tt // LANES, PEER_NKEYS // SUBLANES, 2 * PEER_HEADS, SUBLANES, LANES), F32),
            pltpu.VMEM((2, PEER_NTOP, PEER_HEADS, tt), F32),
            pltpu.VMEM((2, PEER_HEADS, tt), F32),
            pltpu.VMEM((D_MODEL, tt), F32),
            pltpu.VMEM((PEER_EB, tt), F32),
            pltpu.VMEM((PEER_EB, tt), BF16),
        ],
        compiler_params=_cparams("parallel", "arbitrary"),
        name="peer",
    )(h, norm_g, wqt, keys, u, vt, vt)


TOKEN_TILE = 256


def _inproj_kernel(x_ref, g_ref, wr_ref, ww_ref, pr_ref, pw_ref):
    xn = _rms(x_ref[...], g_ref[...]).astype(BF16)
    pr_ref[...] = jnp.dot(xn, wr_ref[...], preferred_element_type=F32)
    pw_ref[...] = jnp.dot(xn, ww_ref[...], preferred_element_type=F32)


def _inproj(x, norm_g, w_ret, w_rw):
    t_total = x.shape[0]
    tm = min(TOKEN_TILE, t_total)
    assert t_total % tm == 0
    const = lambda shape: pl.BlockSpec(shape, lambda t: (0,) * len(shape))
    return pl.pallas_call(
        _inproj_kernel,
        grid=(t_total // tm,),
        in_specs=[pl.BlockSpec((tm, D_MODEL), lambda t: (t, 0)), const((1, D_MODEL)),
                  const((D_MODEL, RET_PROJ)), const((D_MODEL, RWKV_PROJ))],
        out_specs=[pl.BlockSpec((tm, RET_PROJ), lambda t: (t, 0)),
                   pl.BlockSpec((tm, RWKV_PROJ), lambda t: (t, 0))],
        out_shape=[jax.ShapeDtypeStruct((t_total, RET_PROJ), F32),
                   jax.ShapeDtypeStruct((t_total, RWKV_PROJ), F32)],
        compiler_params=_cparams("parallel"),
        name="inproj",
    )(x, norm_g, w_ret, w_rw)


def _rope_partner(x):
    lane = lax.broadcasted_iota(jnp.int32, x.shape, 1)
    up = pltpu.roll(x, x.shape[1] - RET_DK // 2, 1)
    down = pltpu.roll(x, RET_DK // 2, 1)
    return jnp.where(lane % RET_DK < RET_DK // 2, up, down)


def _retention_kernel(p_ref, s0_ref, cos_ref, sin_ref, dmat_ref, xz_ref, ones_ref, o_ref, so_ref, s_ref,
                      *, nb, g_chunk):
    ci = pl.program_id(1)
    qk = RET_HEADS * RET_DK
    n_pairs = RET_HEADS // 2
    row_head = lax.broadcasted_iota(jnp.int32, (LANES, LANES), 0) // RET_DK
    lane_head = lax.broadcasted_iota(jnp.int32, (LANES, LANES), 1) // RET_DV
    same_head = row_head == lane_head

    @pl.when(ci == 0)
    def _init():
        s_ref[...] = jnp.zeros_like(s_ref)
        for p in range(n_pairs):
            s_ref[:, p, :RET_DK, :RET_DV] = s0_ref[:, 2 * p]
            s_ref[:, p, RET_DK:, RET_DV:] = s0_ref[:, 2 * p + 1]

    cos = cos_ref[...]
    sin = sin_ref[...]

    def per_seq(n, carry):
        x = p_ref[n]
        q = x[:, :qk]
        k = x[:, qk:2 * qk]
        q = q * cos + _rope_partner(q) * sin
        k = (k * cos + _rope_partner(k) * sin) * (RET_DK ** -0.5)
        lane_is_hi = lax.broadcasted_iota(jnp.int32, (q.shape[0], LANES), 1) >= RET_DK
        for p in range(n_pairs):
            ls = slice(p * LANES, (p + 1) * LANES)
            qp, kp, vp = q[:, ls], k[:, ls], x[:, 2 * qk + p * LANES:2 * qk + (p + 1) * LANES]
            kb, vb = kp.astype(BF16), vp.astype(BF16)
            s_p = s_ref[n, p]
            o = jnp.dot(qp.astype(BF16), s_p.astype(BF16), preferred_element_type=F32) * xz_ref[0, p]
            for hh in range(2):
                mine = lane_is_hi if hh else jnp.logical_not(lane_is_hi)
                inner = _dot_nt(jnp.where(mine, qp, 0.0).astype(BF16), kb) * dmat_ref[2 * p + hh]
                o = o + jnp.dot(inner.astype(BF16), jnp.where(mine, vp, 0.0).astype(BF16),
                                preferred_element_type=F32)
            o_ref[n, :, ls] = o
            decay = jnp.where(row_head == 0, g_chunk[2 * p], g_chunk[2 * p + 1])
            grown = _dot_tn((kp * xz_ref[1, p]).astype(BF16), vb)
            s_ref[n, p] = s_p * decay + jnp.where(same_head, grown, 0.0)
        o = o_ref[n]
        g = x[:, 2 * qk + RET_W:]
        oc = o - _head_sums(o, ones_ref[...]) * (1.0 / RET_DV)
        var = _head_sums(oc * oc, ones_ref[...]) * (1.0 / RET_DV)
        o_ref[n] = oc * lax.rsqrt(var + GN_EPS) * (g * jax.nn.sigmoid(g))
        return carry

    lax.fori_loop(0, nb, per_seq, 0)

    @pl.when(ci == pl.num_programs(1) - 1)
    def _done():
        for p in range(n_pairs):
            so_ref[:, 2 * p] = s_ref[:, p, :RET_DK, :RET_DV]
            so_ref[:, 2 * p + 1] = s_ref[:, p, RET_DK:, RET_DV:]


def _retention(proj_ret, s0, pos, head_ones):
    assert RET_W == RWKV_W and RET_DV == RWKV_N
    b, l, _ = proj_ret.shape
    c = RET_CHUNK if l % RET_CHUNK == 0 else l
    nb = 1 if c == RET_CHUNK else 16
    assert b % nb == 0
    half = RET_DK // 2
    inv = ROPE_BASE ** (-jnp.arange(half, dtype=F32) / half)
    ang = pos.astype(F32)[:, None] * inv[None, :]
    cos = jnp.tile(jnp.cos(ang), (1, 2 * RET_HEADS))
    sin = jnp.tile(jnp.concatenate([-jnp.sin(ang), jnp.sin(ang)], axis=1), (1, RET_HEADS))
    lg = jnp.log(1.0 - 2.0 ** (-5.0 - jnp.arange(RET_HEADS, dtype=F32)))
    idx = jnp.arange(c, dtype=F32)
    diff = idx[:, None] - idx[None, :]
    dmat = jnp.where(diff[None] >= 0, jnp.exp(jnp.maximum(diff, 0.0)[None] * lg[:, None, None]), 0.0)
    xi = jnp.exp((idx + 1.0)[None, :] * lg[:, None])
    zeta = jnp.exp((c - 1.0 - idx)[None, :] * lg[:, None])
    pair_lanes = lambda a: jnp.repeat(a.reshape(RET_HEADS // 2, 2, c).transpose(0, 2, 1), RET_DV, axis=-1)
    xz = jnp.stack([pair_lanes(xi), pair_lanes(zeta)])
    g_chunk = tuple(math.exp(c * math.log(1.0 - 2.0 ** (-5.0 - h))) for h in range(RET_HEADS))
    kern = functools.partial(_retention_kernel, nb=nb, g_chunk=g_chunk)
    return pl.pallas_call(
        kern,
        grid=(b // nb, l // c),
        in_specs=[
            pl.BlockSpec((nb, c, RET_PROJ), lambda i, j: (i, j, 0)),
            pl.BlockSpec((nb, RET_HEADS, RET_DK, RET_DV), lambda i, j: (i, 0, 0, 0)),
            pl.BlockSpec((c, RET_HEADS * RET_DK), lambda i, j: (j, 0)),
            pl.BlockSpec((c, RET_HEADS * RET_DK), lambda i, j: (j, 0)),
            pl.BlockSpec((RET_HEADS, c, c), lambda i, j: (0, 0, 0)),
            pl.BlockSpec((2, RET_HEADS // 2, c, LANES), lambda i, j: (0, 0, 0, 0)),
            pl.BlockSpec((RET_W, RET_W), lambda i, j: (0, 0)),
        ],
        out_specs=[pl.BlockSpec((nb, c, RET_W), lambda i, j: (i, j, 0)),
                   pl.BlockSpec((nb, RET_HEADS, RET_DK, RET_DV), lambda i, j: (i, 0, 0, 0))],
        out_shape=[jax.ShapeDtypeStruct((b, l, RET_W), F32),
                   jax.ShapeDtypeStruct((b, RET_HEADS, RET_DK, RET_DV), F32)],
        scratch_shapes=[pltpu.VMEM((nb, RET_HEADS // 2, 2 * RET_DK, 2 * RET_DV), F32)],
        compiler_params=_cparams("parallel", "arbitrary"),
        name="retention",
    )(proj_ret, s0, cos, sin, dmat, xz, head_ones)


def _softplus(x):
    return jnp.maximum(x, 0.0) + jnp.log1p(jnp.exp(-jnp.abs(x)))


def _rwkv_prep_kernel(x_ref, sh_ref, mu_ref, w0_ref, w2_ref, a0_ref, a2_ref, g2_ref, kks_ref, ka_ref,
                      rk_ref, ones_ref, kkw_ref, cck_ref, rv_ref, gate_ref, bonus_ref,
                      carry_ref, *, nb, tb):
    ti = pl.program_id(1)
    x = x_ref[...]
    first = jnp.where(ti == 0, sh_ref[...], carry_ref[...])
    prev = pltpu.roll(x.reshape(nb * tb, RWKV_PROJ), 1, 0).reshape(nb, tb, RWKV_PROJ)
    tpos = lax.broadcasted_iota(jnp.int32, x.shape, 1)
    prev = jnp.where(tpos == 0, first, prev)
    carry_ref[...] = x[:, tb - 1:tb, :]
    f = (x + (prev - x) * mu_ref[...]).reshape(nb * tb, RWKV_PROJ)
    o1, o2, o3 = RWKV_W, 2 * RWKV_W, 3 * RWKV_W
    r, k7, v7 = f[:, :o1], f[:, o1:o2], f[:, o2:o3]
    fw = f[:, o3:o3 + DECAY_LORA]
    fa = f[:, o3 + DECAY_LORA:o3 + DECAY_LORA + AAA_LORA]
    fg = f[:, o3 + DECAY_LORA + AAA_LORA:]
    w = -_softplus(-(w0_ref[...] + _dot3(jnp.tanh(fw), w2_ref[...]))) - 0.5
    a = jax.nn.sigmoid(a0_ref[...] + _dot3(fa, a2_ref[...]))
    gate = _dot3(jax.nn.sigmoid(fg), g2_ref[...])
    kk = k7 * kks_ref[...]
    kk = kk / jnp.maximum(jnp.sqrt(_head_sums(kk * kk, ones_ref[...])), 1e-12)
    k = k7 * (1.0 + (a - 1.0) * ka_ref[...])
    out3 = lambda y: y.reshape(nb, tb, RWKV_W)
    kkw_ref[:, :, :RWKV_W] = out3(kk)
    kkw_ref[:, :, RWKV_W:] = out3(jnp.exp(-jnp.exp(w)))
    cck_ref[:, :, :RWKV_W] = out3(kk * a)
    cck_ref[:, :, RWKV_W:] = out3(k)
    rv_ref[:, :, :RWKV_W] = out3(r)
    rv_ref[:, :, RWKV_W:] = out3(v7)
    gate_ref[...] = out3(gate)
    bonus_ref[...] = out3(_head_sums(r * k * rk_ref[...], ones_ref[...]) * v7)


def _rwkv_prep(proj_rw, shift, wt):
    b, l, _ = proj_rw.shape
    tb = min(TOKEN_TILE, l)
    nb = max(1, 128 // tb)
    assert l % tb == 0 and b % nb == 0 and tb % SUBLANES == 0
    const = lambda shape: pl.BlockSpec(shape, lambda i, j: (0,) * len(shape))
    blk = pl.BlockSpec((nb, tb, RWKV_W), lambda i, j: (i, j, 0))
    blk2 = pl.BlockSpec((nb, tb, 2 * RWKV_W), lambda i, j: (i, j, 0))
    kern = functools.partial(_rwkv_prep_kernel, nb=nb, tb=tb)
    return pl.pallas_call(
        kern,
        grid=(b // nb, l // tb),
        in_specs=[
            pl.BlockSpec((nb, tb, RWKV_PROJ), lambda i, j: (i, j, 0)),
            pl.BlockSpec((nb, 1, RWKV_PROJ), lambda i, j: (i, 0, 0)),
            const((1, RWKV_PROJ)), const((1, RWKV_W)), const((DECAY_LORA, RWKV_W)),
            const((1, RWKV_W)), const((AAA_LORA, RWKV_W)), const((GATE_LORA, RWKV_W)),
            const((1, RWKV_W)), const((1, RWKV_W)), const((1, RWKV_W)), const((RWKV_W, RWKV_W)),
        ],
        out_specs=[blk2] * 3 + [blk] * 2,
        out_shape=[jax.ShapeDtypeStruct((b, l, 2 * RWKV_W), F32)] * 3 + [jax.ShapeDtypeStruct((b, l, RWKV_W), F32)] * 2,
        scratch_shapes=[pltpu.VMEM((nb, 1, RWKV_PROJ), F32)],
        compiler_params=_cparams("parallel", "arbitrary"),
        name="rwkv_prep",
    )(proj_rw, shift.reshape(b, 1, RWKV_PROJ), wt['mu'], wt['w0'], wt['w2'], wt['a0'], wt['a2'],
      wt['g2'], wt['kk'], wt['ka'], wt['rk'], wt['head_ones'])


def _wkv_kernel(*refs, tb, ni, paired):
    n_in = 3 if paired else 6
    in_refs, (s0_ref, y_ref, so_ref, s_ref, j_ref, v_ref) = refs[:n_in], refs[n_in:]
    ti = pl.program_id(1)

    @pl.when(ti == 0)
    def _init():
        s_ref[...] = s0_ref[0]

    if paired:
        def spread(t, carry):
            low = lax.broadcasted_iota(jnp.int32, (RWKV_N, LANES), 1) < LANES // 2
            low_v = lax.broadcasted_iota(jnp.int32, (ni, LANES), 1) < LANES // 2
            kkw, cck, rv = (ref[0, t] for ref in in_refs)
            kkw_sw, cck_sw, rv_sw = (pltpu.roll(x, LANES // 2, 1) for x in (kkw, cck, rv))
            j_ref[t, 0], j_ref[t, 1] = jnp.where(low, kkw, kkw_sw), jnp.where(low, kkw_sw, kkw)
            j_ref[t, 2], j_ref[t, 3] = jnp.where(low, cck, cck_sw), jnp.where(low, cck_sw, cck)
            j_ref[t, 4] = jnp.where(low, rv, rv_sw)
            v_ref[t] = jnp.where(low_v, rv_sw[:ni], rv[ni:])
            return carry

        lax.fori_loop(0, tb, spread, 0, unroll=4)

    def step(t, carry):
        if paired:
            kk, w, cc, k, r = (j_ref[t, i] for i in range(5))
            vrow = lambda ii: v_ref[t, ii:ii + 1, :]
        else:
            kk, w, cc, k, r = (ref[0, t] for ref in in_refs[:5])
            vrow = lambda ii: in_refs[5][0, t, ii:ii + 1, :]
        for ii in range(ni):
            s = s_ref[ii]
            sa = -jnp.sum(s * kk, axis=0, keepdims=True)
            sn = s * w + sa * cc + vrow(ii) * k
            s_ref[ii] = sn
            y_ref[0, t, ii:ii + 1, :] = jnp.sum(sn * r, axis=0, keepdims=True)
        return carry

    lax.fori_loop(0, tb, step, 0)

    @pl.when(ti == pl.num_programs(1) - 1)
    def _done():
        so_ref[0] = s_ref[...]


def _wkv_scan(inputs, s0):
    paired = len(inputs) == 3
    g, ni = s0.shape[:2]
    l = inputs[0].shape[1]
    tb = min(32, l)
    assert l % tb == 0 and (not paired or 2 * ni == RWKV_N)
    jblk = pl.BlockSpec((1, tb, RWKV_N, LANES), lambda i, j: (i, j, 0, 0))
    iblk = pl.BlockSpec((1, tb, ni, LANES), lambda i, j: (i, j, 0, 0))
    sblk = pl.BlockSpec((1, ni, RWKV_N, LANES), lambda i, j: (i, 0, 0, 0))
    kern = functools.partial(_wkv_kernel, tb=tb, ni=ni, paired=paired)
    return pl.pallas_call(
        kern,
        grid=(g, l // tb),
        in_specs=([jblk] * 3 if paired else [jblk] * 5 + [iblk]) + [sblk],
        out_specs=[iblk, sblk],
        out_shape=[jax.ShapeDtypeStruct((g, l, ni, LANES), F32),
                   jax.ShapeDtypeStruct((g, ni, RWKV_N, LANES), F32)],
        scratch_shapes=[pltpu.VMEM((ni, RWKV_N, LANES), F32),
                        pltpu.VMEM((tb if paired else 1, 5, RWKV_N, LANES), F32),
                        pltpu.VMEM((tb if paired else 1, ni, LANES), F32)],
        compiler_params=_cparams("parallel", "arbitrary"),
        name="wkv_scan",
    )(*inputs, s0)


def _outproj_kernel(x_ref, oret_ref, y_ref, bonus_ref, gate_ref, lng_ref, lnb_ref, ones_ref,
                    wtop_ref, wbot_ref, o_ref):
    y = y_ref[...]
    yc = y - _head_sums(y, ones_ref[...]) * (1.0 / RWKV_N)
    var = _head_sums(yc * yc, ones_ref[...]) * (1.0 / RWKV_N)
    o = yc * lax.rsqrt(var + RWKV_LN_EPS) * lng_ref[...] + lnb_ref[...]
    o = (o + bonus_ref[...]) * gate_ref[...]
    o_ref[...] = (x_ref[...]
                  + jnp.dot(oret_ref[...].astype(BF16), wtop_ref[...], preferred_element_type=F32)
                  + jnp.dot(o.astype(BF16), wbot_ref[...], preferred_element_type=F32))


def _outproj(x, o_ret, y, bonus, gate, wt):
    t_total = x.shape[0]
    tm = min(TOKEN_TILE, t_total)
    const = lambda shape: pl.BlockSpec(shape, lambda t: (0,) * len(shape))
    half = pl.BlockSpec((tm, RWKV_W), lambda t: (t, 0))
    full = pl.BlockSpec((tm, D_MODEL), lambda t: (t, 0))
    return pl.pallas_call(
        _outproj_kernel,
        grid=(t_total // tm,),
        in_specs=[full, half, half, half, half, const((1, RWKV_W)), const((1, RWKV_W)),
                  const((RWKV_W, RWKV_W)), const((RET_W, D_MODEL)), const((RWKV_W, D_MODEL))],
        out_specs=full,
        out_shape=jax.ShapeDtypeStruct((t_total, D_MODEL), F32),
        compiler_params=_cparams("parallel"),
        name="outproj",
    )(x, o_ret, y, bonus, gate, wt['ln_g'], wt['ln_b'], wt['head_ones'], wt['w_out_top'], wt['w_out_bot'])


def _ple_kernel(h_ref, p_ref, gn_ref, wg_ref, wp_ref, fn_ref, o_ref):
    h = h_ref[...]
    gate = jax.nn.sigmoid(jnp.dot(_rms(h, gn_ref[...]).astype(BF16), wg_ref[...], preferred_element_type=F32))
    h = h + gate * jnp.dot(p_ref[...].astype(BF16), wp_ref[...], preferred_element_type=F32)
    o_ref[...] = _rms(h, fn_ref[...])


def _ple(h, p, wt):
    t_total = h.shape[0]
    tm = min(TOKEN_TILE, t_total)
    const = lambda shape: pl.BlockSpec(shape, lambda t: (0,) * len(shape))
    full = pl.BlockSpec((tm, D_MODEL), lambda t: (t, 0))
    return pl.pallas_call(
        _ple_kernel,
        grid=(t_total // tm,),
        in_specs=[full, pl.BlockSpec((tm, PLE_DIM), lambda t: (t, 0)), const((1, D_MODEL)),
                  const((D_MODEL, D_MODEL)), const((PLE_DIM, D_MODEL)), const((1, D_MODEL))],
        out_specs=full,
        out_shape=jax.ShapeDtypeStruct((t_total, D_MODEL), F32),
        compiler_params=_cparams("parallel"),
        name="ple",
    )(h, p, wt['ple_norm'], wt['ple_gate'], wt['ple_proj'], wt['norm_final'])


def _prep_weights(p):
    row = lambda a: a.reshape(1, -1)
    lane_head = jnp.arange(RWKV_W) // RWKV_N
    return {
        'norm_mix': row(p['norm_mix'][0]),
        'w_ret': p['w_in'][0][:, :RET_PROJ].astype(BF16),
        'w_rw': p['w_in'][0][:, RET_PROJ:].astype(BF16),
        'mu': row(p['rw_mu'][0]), 'w0': row(p['rw_w0'][0]), 'w2': p['rw_w2'][0],
        'a0': row(p['rw_a0'][0]), 'a2': p['rw_a2'][0], 'g2': p['rw_g2'][0],
        'kk': row(p['rw_kk'][0]), 'ka': row(p['rw_ka'][0]), 'rk': row(p['rw_rk'][0]),
        'ln_g': row(p['rw_ln_g'][0]), 'ln_b': row(p['rw_ln_b'][0]),
        'head_ones': (lane_head[:, None] == lane_head[None, :]).astype(BF16),
        'w_out_top': p['w_out'][0][:RET_W].astype(BF16),
        'w_out_bot': p['w_out'][0][RET_W:].astype(BF16),
        'norm_ffn': row(p['norm_ffn'][0]),
        'wqt': jnp.stack(_split(p['peer_wq'][0].T)),
        'keys': jnp.stack(_split(p['peer_keys'][0].reshape(2 * PEER_HEADS, PEER_NKEYS, PEER_DKEY // 2))),
        'u': p['peer_u'][0].astype(BF16),
        'vt': p['peer_v'][0].astype(BF16).T,
        'ple_norm': row(p['ple_norm'][0]),
        'ple_gate': p['ple_gate'][0].astype(BF16),
        'ple_proj': p['ple_proj'][0].astype(BF16),
        'norm_final': row(p['norm_final']),
    }


def _scan_inputs(pairs, b):
    l = pairs[0].shape[1]
    if b * RWKV_HEADS * 2 == LANES:
        return [p.reshape(b, l, 2, RWKV_HEADS, RWKV_N).transpose(1, 4, 2, 0, 3).reshape(1, l, RWKV_N, LANES)
                for p in pairs]
    assert b == LANES
    split = lambda p: [h.reshape(b, l, RWKV_HEADS, RWKV_N).transpose(2, 1, 3, 0)
                       for h in (p[..., :RWKV_W], p[..., RWKV_W:])]
    (kk, w), (cc, k), (r, v) = (split(p) for p in pairs)
    return [kk, w, cc, k, r, v]


def _values_from_lanes(y, b):
    g, l, ni, _ = y.shape
    if g == 1:
        y = y.reshape(l, ni, 2, b, RWKV_HEADS).transpose(3, 0, 4, 2, 1)
    else:
        y = y.transpose(3, 1, 0, 2)
    return y.reshape(b, l, RWKV_W)


def _state_to_lanes(s):
    b = s.shape[0]
    if b * RWKV_HEADS * 2 == LANES:
        s = s.reshape(b, RWKV_HEADS, 2, RWKV_N // 2, RWKV_N).transpose(3, 4, 2, 0, 1)
        return s.reshape(1, RWKV_N // 2, RWKV_N, LANES)
    return s.transpose(1, 2, 3, 0)


def _state_from_lanes(s, b):
    g, ni, _, _ = s.shape
    if g == 1:
        s = s.reshape(ni, RWKV_N, 2, b, RWKV_HEADS).transpose(3, 4, 2, 0, 1)
        return s.reshape(b, RWKV_HEADS, RWKV_N, RWKV_N)
    return s.transpose(3, 0, 1, 2)


def _layer(x, p_l, pos, s_ret, s_wkv, s_shift, wt):
    b, l, _ = x.shape
    x2 = x.reshape(b * l, D_MODEL)
    proj_ret, proj_rw = _inproj(x2, wt['norm_mix'], wt['w_ret'], wt['w_rw'])
    o_ret, s_ret_new = _retention(proj_ret.reshape(b, l, RET_PROJ), s_ret, pos, wt['head_ones'])
    proj_rw = proj_rw.reshape(b, l, RWKV_PROJ)
    new_shift = proj_rw[:, -1]
    kkw, cck, rv, gate, bonus = _rwkv_prep(proj_rw, s_shift, wt)
    y, s_lanes = _wkv_scan(_scan_inputs((kkw, cck, rv), b), _state_to_lanes(s_wkv))
    y = _values_from_lanes(y, b)
    s_wkv_new = _state_from_lanes(s_lanes, b)
    flat = lambda a: a.reshape(b * l, -1)
    h = _outproj(x2, flat(o_ret), flat(y), flat(bonus), flat(gate), wt)
    h = _peer(h, wt['norm_ffn'], wt['wqt'], wt['keys'], wt['u'], wt['vt'])
    y_out = _ple(h, p_l.reshape(b * l, PLE_DIM), wt)
    return y_out.reshape(b, l, D_MODEL), s_ret_new, s_wkv_new, new_shift


def kernel(x_prompt, x_sample, state_ret, state_wkv, state_shift, p_prompt, p_sample, norm_mix, w_in, rw_mu, rw_w0, rw_w2, rw_a0, rw_a2, rw_g2, rw_kk, rw_ka, rw_rk, rw_ln_g, rw_ln_b, w_out, norm_ffn, peer_wq, peer_keys, peer_u, peer_v, ple_norm, ple_gate, ple_proj, norm_final):
    assert norm_mix.shape[0] == 1, "single-layer trunk"
    wt = _prep_weights(dict(
        norm_mix=norm_mix, w_in=w_in, rw_mu=rw_mu, rw_w0=rw_w0, rw_w2=rw_w2, rw_a0=rw_a0, rw_a2=rw_a2,
        rw_g2=rw_g2, rw_kk=rw_kk, rw_ka=rw_ka, rw_rk=rw_rk, rw_ln_g=rw_ln_g, rw_ln_b=rw_ln_b, w_out=w_out,
        norm_ffn=norm_ffn, peer_wq=peer_wq, peer_keys=peer_keys, peer_u=peer_u, peer_v=peer_v,
        ple_norm=ple_norm, ple_gate=ple_gate, ple_proj=ple_proj, norm_final=norm_final))
    bp, lp, _ = x_prompt.shape
    bs, ls, _ = x_sample.shape
    pos_p = jnp.arange(lp, dtype=jnp.int32)
    pos_s = PAST_LEN + jnp.arange(ls, dtype=jnp.int32)
    zero_ret = jnp.zeros((bp, RET_HEADS, RET_DK, RET_DV), F32)
    zero_wkv = jnp.zeros((bp, RWKV_HEADS, RWKV_N, RWKV_N), F32)
    zero_shift = jnp.zeros((bp, RWKV_PROJ), F32)
    yp, rp, wp, sp = _layer(x_prompt, p_prompt[0], pos_p, zero_ret, zero_wkv, zero_shift, wt)
    ys, rs, ws, ss = _layer(x_sample, p_sample[0], pos_s, state_ret[0], state_wkv[0], state_shift[0], wt)
    return (yp, ys, rp[None], wp[None], sp[None], rs[None], ws[None], ss[None])
```

```python
import functools
import math

import jax
import jax.numpy as jnp
from jax import lax
from jax.experimental import pallas as pl
from jax.experimental.pallas import tpu as pltpu

F32 = jnp.float32
BF16 = jnp.bfloat16

D_MODEL = 1024
PAST_LEN = 16384
RET_HEADS = 8
RET_DK = 64
RET_DV = 64
RET_W = RET_HEADS * RET_DV
RWKV_HEADS = 8
RWKV_N = 64
RWKV_W = RWKV_HEADS * RWKV_N
DECAY_LORA = 64
AAA_LORA = 64
GATE_LORA = 128
RET_PROJ = 2 * RET_HEADS * RET_DK + 2 * RET_W
RWKV_PROJ = 3 * RWKV_W + DECAY_LORA + AAA_LORA + GATE_LORA
RET_CHUNK = 128
ROPE_BASE = 10000.0
PEER_HEADS = 8
PEER_NKEYS = 128
PEER_EXPERTS = PEER_NKEYS * PEER_NKEYS
PEER_DKEY = 256
PEER_TOPK = 16
PLE_DIM = 256
RMS_EPS = 1e-6
GN_EPS = 1e-5
RWKV_LN_EPS = 64e-5

LANES = 128
SUBLANES = 8
BF16_ROWS = 16
VMEM_LIMIT = 56 * 1024 * 1024

NEG_INF = float("-inf")


def _cparams(*sem):
    return pltpu.CompilerParams(dimension_semantics=sem, vmem_limit_bytes=VMEM_LIMIT)


def _rms(x, g):
    return x * lax.rsqrt(jnp.mean(x * x, axis=-1, keepdims=True) + RMS_EPS) * g


def _split(x):
    hi = x.astype(BF16)
    return hi, (x - hi.astype(F32)).astype(BF16)


def _dot3(a, b):
    ah, al = _split(a)
    bh, bl = _split(b)
    dot = functools.partial(jnp.dot, preferred_element_type=F32)
    return dot(ah, bh) + dot(ah, bl) + dot(al, bh)


def _head_sums(x, head_ones):
    xh, xl = _split(x)
    dot = functools.partial(jnp.dot, preferred_element_type=F32)
    return dot(xh, head_ones) + dot(xl, head_ones)


def _dot_nt(a, b):
    return lax.dot_general(a, b, (((1,), (1,)), ((), ())), preferred_element_type=F32)


def _dot_tn(a, b):
    return lax.dot_general(a, b, (((0,), (0,)), ((), ())), preferred_element_type=F32)


PEER_TT = 512
PEER_ROWS = 8
PEER_EB = PEER_ROWS * PEER_NKEYS
PEER_CHUNK = 256
PEER_NTOP = PEER_TOPK + 1
_CANDS = tuple((a, b) for a in range(PEER_NTOP) for b in range(PEER_NTOP)
               if (a + 1) * (b + 1) <= PEER_NTOP)


def _sorting_network(n):
    pairs = []
    p = 1
    while p < n:
        k = p
        while k >= 1:
            for j in range(k % p, n - k, 2 * k):
                for i in range(min(k, n - j - k)):
                    if (i + j) // (2 * p) == (i + j + k) // (2 * p):
                        pairs.append((i + j, i + j + k))
            k //= 2
        p *= 2
    return tuple(pairs)


def _top_values(x, n):
    n_tiles = x.shape[0] // SUBLANES
    rows = [x[r * SUBLANES:(r + 1) * SUBLANES] for r in range(n_tiles)]
    for a, b in _sorting_network(n_tiles):
        rows[a], rows[b] = jnp.maximum(rows[a], rows[b]), jnp.minimum(rows[a], rows[b])
    out = []
    for r in range(n):
        m = jnp.max(rows[0], axis=0, keepdims=True)
        out.append(m)
        taken = rows[0] == m
        for q in range(min(n_tiles, n - 1 - r)):
            below = rows[q + 1] if q + 1 < n_tiles else NEG_INF
            rows[q] = jnp.where(taken, below, rows[q])
    return out


def _peer_kernel(h_ref, g_ref, wqt_ref, keys_ref, u_ref, vt_ref, vprev_ref, o_ref,
                 xn_ref, s_ref, e1_ref, se2_ref, top_ref, tau_ref, acc_ref, ht_ref, act_ref, *, tt):
    eb = pl.program_id(1)
    nlb = tt // LANES
    n_ck = PEER_EB // PEER_CHUNK
    chunk = lambda ck: slice(ck * PEER_CHUNK, (ck + 1) * PEER_CHUNK)

    @pl.when(eb == 0)
    def _prepare():
        xn = _rms(h_ref[...], g_ref[...])
        xh, xl = _split(xn)
        xn_ref[...] = (xn * (2.0 ** -0.5)).astype(BF16)
        qt = (_dot_nt(wqt_ref[0], xh) + _dot_nt(wqt_ref[0], xl) + _dot_nt(wqt_ref[1], xh))
        for hc in range(2 * PEER_HEADS):
            qh, ql = _split(qt[hc * PEER_NKEYS:(hc + 1) * PEER_NKEYS, :])
            s_ref[hc] = (jnp.dot(keys_ref[0, hc], qh, preferred_element_type=F32)
                         + jnp.dot(keys_ref[0, hc], ql, preferred_element_type=F32)
                         + jnp.dot(keys_ref[1, hc], qh, preferred_element_type=F32))

        def per_lane_block(lb, carry):
            lanes = pl.ds(pl.multiple_of(lb * LANES, LANES), LANES)
            for hc in range(2 * PEER_HEADS):
                h, c = divmod(hc, 2)
                work = s_ref[hc, :, lanes]
                ex = jnp.exp(work - jnp.max(work, axis=0, keepdims=True))
                if c == 0:
                    e1_ref[h, :, lanes] = ex
                else:
                    for rb in range(PEER_NKEYS // SUBLANES):
                        rows = slice(rb * SUBLANES, (rb + 1) * SUBLANES)
                        se2_ref[lb, rb, 2 * h] = work[rows]
                        se2_ref[lb, rb, 2 * h + 1] = ex[rows]
                for r, m in enumerate(_top_values(work, PEER_NTOP)):
                    top_ref[c, r, pl.ds(h, 1), lanes] = m
            v1 = [top_ref[0, r, :, lanes] for r in range(PEER_NTOP)]
            v2 = [top_ref[1, r, :, lanes] for r in range(PEER_NTOP)]
            cands = [v1[a] + v2[b] for (a, b) in _CANDS]
            cv = []
            for r in range(PEER_TOPK + 1):
                m = functools.reduce(jnp.maximum, cands)
                cv.append(m)
                if r < PEER_TOPK:
                    cands = [jnp.where(x == m, NEG_INF, x) for x in cands]
            z = functools.reduce(lambda a, b: a + b, [jnp.exp(x - cv[0]) for x in cv[:PEER_TOPK]])
            tau_ref[0, :, lanes] = 0.5 * (cv[PEER_TOPK - 1] + cv[PEER_TOPK])
            tau_ref[1, :, lanes] = (2.0 ** -0.5) / z
            return carry

        lax.fori_loop(0, nlb, per_lane_block, 0)
        acc_ref[...] = jnp.zeros_like(acc_ref)
        act_ref[PEER_EB // 2:, :] = jnp.zeros((PEER_EB // 2, tt), BF16)

    irows =pl.ds(pl.multiple_of(eb * PEER_ROWS, PEER_ROWS), PEER_ROWS)
    shape = (SUBLANES, LANES)

    def gate_rows(il, lb):
        lanes = slice(lb * LANES, (lb + 1) * LANES)
        tau = tau_ref[0, :, lanes]
        rz = tau_ref[1, :, lanes]
        theta, e1 = [], []
        for h in range(PEER_HEADS):
            theta.append(jnp.broadcast_to(tau[h:h + 1] - s_ref[2 * h, irows, lanes][il:il + 1], shape))
            e1.append(jnp.broadcast_to(e1_ref[h, irows, lanes][il:il + 1] * rz[h:h + 1], shape))
        for jb in range(PEER_NKEYS // BF16_ROWS):
            pieces = []
            for half in range(2):
                rb = 2 * jb + half
                j0 = rb * SUBLANES
                terms = [jnp.where(se2_ref[lb, rb, 2 * h] >= theta[h],
                                   e1[h] * se2_ref[lb, rb, 2 * h + 1], 0.0) for h in range(PEER_HEADS)]
                while len(terms) > 1:
                    terms = [terms[i] + terms[i + 1] for i in range(0, len(terms), 2)]
                r0 = il * PEER_NKEYS + j0
                hv = ht_ref[r0:r0 + SUBLANES, lanes]
                pieces.append(hv * (1.0 + lax.erf(hv)) * terms[0])
            a0 = il * PEER_NKEYS + jb * BF16_ROWS
            act_ref[a0:a0 + BF16_ROWS, lanes] = jnp.concatenate(pieces, axis=0).astype(BF16)

    lo, hi = slice(0, PEER_EB // 2), slice(PEER_EB // 2, PEER_EB)

    def hval(ck):
        ht_ref[chunk(ck), :] = _dot_nt(u_ref[chunk(ck), :], xn_ref[...])

    def gates(cks):
        for ck in cks:
            for il in range(ck * PEER_CHUNK // PEER_NKEYS, (ck + 1) * PEER_CHUNK // PEER_NKEYS):
                for lb in range(nlb):
                    gate_rows(il, lb)

    first, second = range(n_ck // 2), range(n_ck // 2, n_ck)
    for ck in first:
        hval(ck)
    acc_ref[...] += jnp.dot(vprev_ref[...], act_ref[hi, :], preferred_element_type=F32)
    for ck in second:
        hval(ck)
    gates(first)
    acc_ref[...] += jnp.dot(vt_ref[:, lo], act_ref[lo, :], preferred_element_type=F32)
    gates(second)

    @pl.when(eb == pl.num_programs(1) - 1)
    def _finish():
        tail = jnp.dot(vt_ref[:, hi], act_ref[hi, :], preferred_element_type=F32)
        o_ref[...] = h_ref[...] + (acc_ref[...] + tail).T


def _peer(h, norm_g, wqt, keys, u, vt):
    t_total = h.shape[0]
    tt = min(PEER_TT, t_total)
    assert t_total % tt == 0 and tt % LANES == 0
    n_eb = PEER_EXPERTS // PEER_EB
    n_ck = PEER_EB // PEER_CHUNK
    kern = functools.partial(_peer_kernel, tt=tt)
    return pl.pallas_call(
        kern,
        grid=(t_total // tt, n_eb),
        in_specs=[
            pl.BlockSpec((tt, D_MODEL), lambda t, e: (t, 0)),
            pl.BlockSpec((1, D_MODEL), lambda t, e: (0, 0)),
            pl.BlockSpec((2, PEER_HEADS * PEER_DKEY, D_MODEL), lambda t, e: (0, 0, 0)),
            pl.BlockSpec((2, 2 * PEER_HEADS, PEER_NKEYS, PEER_DKEY // 2), lambda t, e: (0, 0, 0, 0)),
            pl.BlockSpec((PEER_EB, D_MODEL), lambda t, e: (e, 0)),
            pl.BlockSpec((D_MODEL, PEER_EB), lambda t, e: (0, e)),
            pl.BlockSpec((D_MODEL, PEER_EB // 2), lambda t, e: (0, jnp.maximum(2 * e - 1, 0))),
        ],
        out_specs=pl.BlockSpec((tt, D_MODEL), lambda t, e: (t, 0)),
        out_shape=jax.ShapeDtypeStruct((t_total, D_MODEL), F32),
        scratch_shapes=[
            pltpu.VMEM((tt, D_MODEL), BF16),
            pltpu.VMEM((2 * PEER_HEADS, PEER_NKEYS, tt), F32),
            pltpu.VMEM((PEER_HEADS, PEER_NKEYS, tt), F32),
            pltpu.VMEM((tt // LANES, PEER_NKEYS // SUBLANES, 2 * PEER_HEADS, SUBLANES, LANES), F32),
            pltpu.VMEM((2, PEER_NTOP, PEER_HEADS, tt), F32),
            pltpu.VMEM((2, PEER_HEADS, tt), F32),
            pltpu.VMEM((D_MODEL, tt), F32),
            pltpu.VMEM((PEER_EB, tt), F32),
            pltpu.VMEM((PEER_EB, tt), BF16),
        ],
        compiler_params=_cparams("parallel", "arbitrary"),
        name="peer",
    )(h, norm_g, wqt, keys, u, vt, vt)


TOKEN_TILE = 512


def _inproj_kernel(x_ref, g_ref, wr_ref, ww_ref, pr_ref, pw_ref):
    xn = _rms(x_ref[...], g_ref[...]).astype(BF16)
    pr_ref[...] = jnp.dot(xn, wr_ref[...], preferred_element_type=F32)
    pw_ref[...] = jnp.dot(xn, ww_ref[...], preferred_element_type=F32)


def _inproj(x, norm_g, w_ret, w_rw):
    t_total = x.shape[0]
    tm = min(TOKEN_TILE, t_total)
    assert t_total % tm == 0
    const = lambda shape: pl.BlockSpec(shape, lambda t: (0,) * len(shape))
    return pl.pallas_call(
        _inproj_kernel,
        grid=(t_total // tm,),
        in_specs=[pl.BlockSpec((tm, D_MODEL), lambda t: (t, 0)), const((1, D_MODEL)),
                  const((D_MODEL, RET_PROJ)), const((D_MODEL, RWKV_PROJ))],
        out_specs=[pl.BlockSpec((tm, RET_PROJ), lambda t: (t, 0)),
                   pl.BlockSpec((tm, RWKV_PROJ), lambda t: (t, 0))],
        out_shape=[jax.ShapeDtypeStruct((t_total, RET_PROJ), F32),
                   jax.ShapeDtypeStruct((t_total, RWKV_PROJ), F32)],
        compiler_params=_cparams("parallel"),
        name="inproj",
    )(x, norm_g, w_ret, w_rw)


def _rope_partner(x):
    lane = lax.broadcasted_iota(jnp.int32, x.shape, 1)
    up = pltpu.roll(x, x.shape[1] - RET_DK // 2, 1)
    down = pltpu.roll(x, RET_DK // 2, 1)
    return jnp.where(lane % RET_DK < RET_DK // 2, up, down)


def _retention_kernel(p_ref, s0_ref, cos_ref, sin_ref, dmat_ref, xz_ref, ones_ref, o_ref, so_ref, s_ref,
                      *, nb, g_chunk):
    ci = pl.program_id(1)
    qk = RET_HEADS * RET_DK
    n_pairs = RET_HEADS // 2
    row_head = lax.broadcasted_iota(jnp.int32, (LANES, LANES), 0) // RET_DK
    lane_head = lax.broadcasted_iota(jnp.int32, (LANES, LANES), 1) // RET_DV
    same_head = row_head == lane_head

    @pl.when(ci == 0)
    def _init():
        s_ref[...] = jnp.zeros_like(s_ref)
        for p in range(n_pairs):
            s_ref[:, p, :RET_DK, :RET_DV] = s0_ref[:, 2 * p]
            s_ref[:, p, RET_DK:, RET_DV:] = s0_ref[:, 2 * p + 1]

    cos = cos_ref[...]
    sin = sin_ref[...]

    def per_seq(n, carry):
        x = p_ref[n]
        q = x[:, :qk]
        k = x[:, qk:2 * qk]
        q = q * cos + _rope_partner(q) * sin
        k = (k * cos + _rope_partner(k) * sin) * (RET_DK ** -0.5)
        lane_is_hi = lax.broadcasted_iota(jnp.int32, (q.shape[0], LANES), 1) >= RET_DK
        for p in range(n_pairs):
            ls = slice(p * LANES, (p + 1) * LANES)
            qp, kp, vp = q[:, ls], k[:, ls], x[:, 2 * qk + p * LANES:2 * qk + (p + 1) * LANES]
            kb, vb = kp.astype(BF16), vp.astype(BF16)
            s_p = s_ref[n, p]
            o = jnp.dot(qp.astype(BF16), s_p.astype(BF16), preferred_element_type=F32) * xz_ref[0, p]
            for hh in range(2):
                mine = lane_is_hi if hh else jnp.logical_not(lane_is_hi)
                inner = _dot_nt(jnp.where(mine, qp, 0.0).astype(BF16), kb) * dmat_ref[2 * p + hh]
                o = o + jnp.dot(inner.astype(BF16), jnp.where(mine, vp, 0.0).astype(BF16),
                                preferred_element_type=F32)
            o_ref[n, :, ls] = o
            decay = jnp.where(row_head == 0, g_chunk[2 * p], g_chunk[2 * p + 1])
            grown = _dot_tn((kp * xz_ref[1, p]).astype(BF16), vb)
            s_ref[n, p] = s_p * decay + jnp.where(same_head, grown, 0.0)
        o = o_ref[n]
        g = x[:, 2 * qk + RET_W:]
        oc = o - _head_sums(o, ones_ref[...]) * (1.0 / RET_DV)
        var = _head_sums(oc * oc, ones_ref[...]) * (1.0 / RET_DV)
        o_ref[n] = oc * lax.rsqrt(var + GN_EPS) * (g * jax.nn.sigmoid(g))
        return carry

    lax.fori_loop(0, nb, per_seq, 0)

    @pl.when(ci == pl.num_programs(1) - 1)
    def _done():
        for p in range(n_pairs):
            so_ref[:, 2 * p] = s_ref[:, p, :RET_DK, :RET_DV]
            so_ref[:, 2 * p + 1] = s_ref[:, p, RET_DK:, RET_DV:]


def _retention(proj_ret, s0, pos, head_ones):
    assert RET_W == RWKV_W and RET_DV == RWKV_N
    b, l, _ = proj_ret.shape
    c = RET_CHUNK if l % RET_CHUNK == 0 else l
    nb = 1 if c == RET_CHUNK else 16
    assert b % nb == 0
    half = RET_DK // 2
    inv = ROPE_BASE ** (-jnp.arange(half, dtype=F32) / half)
    ang = pos.astype(F32)[:, None] * inv[None, :]
    cos = jnp.tile(jnp.cos(ang), (1, 2 * RET_HEADS))
    sin = jnp.tile(jnp.concatenate([-jnp.sin(ang), jnp.sin(ang)], axis=1), (1, RET_HEADS))
    lg = jnp.log(1.0 - 2.0 ** (-5.0 - jnp.arange(RET_HEADS, dtype=F32)))
    idx = jnp.arange(c, dtype=F32)
    diff = idx[:, None] - idx[None, :]
    dmat = jnp.where(diff[None] >= 0, jnp.exp(jnp.maximum(diff, 0.0)[None] * lg[:, None, None]), 0.0)
    xi = jnp.exp((idx + 1.0)[None, :] * lg[:, None])
    zeta = jnp.exp((c - 1.0 - idx)[None, :] * lg[:, None])
    pair_lanes = lambda a: jnp.repeat(a.reshape(RET_HEADS // 2, 2, c).transpose(0, 2, 1), RET_DV, axis=-1)
    xz = jnp.stack([pair_lanes(xi), pair_lanes(zeta)])
    g_chunk = tuple(math.exp(c * math.log(1.0 - 2.0 ** (-5.0 - h))) for h in range(RET_HEADS))
    kern = functools.partial(_retention_kernel, nb=nb, g_chunk=g_chunk)
    return pl.pallas_call(
        kern,
        grid=(b // nb, l // c),
        in_specs=[
            pl.BlockSpec((nb, c, RET_PROJ), lambda i, j: (i, j, 0)),
            pl.BlockSpec((nb, RET_HEADS, RET_DK, RET_DV), lambda i, j: (i, 0, 0, 0)),
            pl.BlockSpec((c, RET_HEADS * RET_DK), lambda i, j: (j, 0)),
            pl.BlockSpec((c, RET_HEADS * RET_DK), lambda i, j: (j, 0)),
            pl.BlockSpec((RET_HEADS, c, c), lambda i, j: (0, 0, 0)),
            pl.BlockSpec((2, RET_HEADS // 2, c, LANES), lambda i, j: (0, 0, 0, 0)),
            pl.BlockSpec((RET_W, RET_W), lambda i, j: (0, 0)),
        ],
        out_specs=[pl.BlockSpec((nb, c, RET_W), lambda i, j: (i, j, 0)),
                   pl.BlockSpec((nb, RET_HEADS, RET_DK, RET_DV), lambda i, j: (i, 0, 0, 0))],
        out_shape=[jax.ShapeDtypeStruct((b, l, RET_W), F32),
                   jax.ShapeDtypeStruct((b, RET_HEADS, RET_DK, RET_DV), F32)],
        scratch_shapes=[pltpu.VMEM((nb, RET_HEADS // 2, 2 * RET_DK, 2 * RET_DV), F32)],
        compiler_params=_cparams("parallel", "arbitrary"),
        name="retention",
    )(proj_ret, s0, cos, sin, dmat, xz, head_ones)


def _softplus(x):
    return jnp.maximum(x, 0.0) + jnp.log1p(jnp.exp(-jnp.abs(x)))


def _rwkv_prep_kernel(x_ref, sh_ref, mu_ref, w0_ref, w2_ref, a0_ref, a2_ref, g2_ref, kks_ref, ka_ref,
                      rk_ref, ones_ref, kkw_ref, cck_ref, rv_ref, gate_ref, bonus_ref,
                      carry_ref, *, nb, tb):
    ti = pl.program_id(1)
    x = x_ref[...]
    first = jnp.where(ti == 0, sh_ref[...], carry_ref[...])
    prev = pltpu.roll(x.reshape(nb * tb, RWKV_PROJ), 1, 0).reshape(nb, tb, RWKV_PROJ)
    tpos = lax.broadcasted_iota(jnp.int32, x.shape, 1)
    prev = jnp.where(tpos == 0, first, prev)
    carry_ref[...] = x[:, tb - 1:tb, :]
    f = (x + (prev - x) * mu_ref[...]).reshape(nb * tb, RWKV_PROJ)
    o1, o2, o3 = RWKV_W, 2 * RWKV_W, 3 * RWKV_W
    r, k7, v7 = f[:, :o1], f[:, o1:o2], f[:, o2:o3]
    fw = f[:, o3:o3 + DECAY_LORA]
    fa = f[:, o3 + DECAY_LORA:o3 + DECAY_LORA + AAA_LORA]
    fg = f[:, o3 + DECAY_LORA + AAA_LORA:]
    w = -_softplus(-(w0_ref[...] + _dot3(jnp.tanh(fw), w2_ref[...]))) - 0.5
    a = jax.nn.sigmoid(a0_ref[...] + _dot3(fa, a2_ref[...]))
    gate = _dot3(jax.nn.sigmoid(fg), g2_ref[...])
    kk = k7 * kks_ref[...]
    kk = kk / jnp.maximum(jnp.sqrt(_head_sums(kk * kk, ones_ref[...])), 1e-12)
    k = k7 * (1.0 + (a - 1.0) * ka_ref[...])
    out3 = lambda y: y.reshape(nb, tb, RWKV_W)
    kkw_ref[:, :, :RWKV_W] = out3(kk)
    kkw_ref[:, :, RWKV_W:] = out3(jnp.exp(-jnp.exp(w)))
    cck_ref[:, :, :RWKV_W] = out3(kk * a)
    cck_ref[:, :, RWKV_W:] = out3(k)
    rv_ref[:, :, :RWKV_W] = out3(r)
    rv_ref[:, :, RWKV_W:] = out3(v7)
    gate_ref[...] = out3(gate)
    bonus_ref[...] = out3(_head_sums(r * k * rk_ref[...], ones_ref[...]) * v7)


def _rwkv_prep(proj_rw, shift, wt):
    b, l, _ = proj_rw.shape
    tb = min(TOKEN_TILE, l)
    nb = max(1, 128 // tb)
    assert l % tb == 0 and b % nb == 0 and tb % SUBLANES == 0
    const = lambda shape: pl.BlockSpec(shape, lambda i, j: (0,) * len(shape))
    blk = pl.BlockSpec((nb, tb, RWKV_W), lambda i, j: (i, j, 0))
    blk2 = pl.BlockSpec((nb, tb, 2 * RWKV_W), lambda i, j: (i, j, 0))
    kern = functools.partial(_rwkv_prep_kernel, nb=nb, tb=tb)
    return pl.pallas_call(
        kern,
        grid=(b // nb, l // tb),
        in_specs=[
            pl.BlockSpec((nb, tb, RWKV_PROJ), lambda i, j: (i, j, 0)),
            pl.BlockSpec((nb, 1, RWKV_PROJ), lambda i, j: (i, 0, 0)),
            const((1, RWKV_PROJ)), const((1, RWKV_W)), const((DECAY_LORA, RWKV_W)),
            const((1, RWKV_W)), const((AAA_LORA, RWKV_W)), const((GATE_LORA, RWKV_W)),
            const((1, RWKV_W)), const((1, RWKV_W)), const((1, RWKV_W)), const((RWKV_W, RWKV_W)),
        ],
        out_specs=[blk2] * 3 + [blk] * 2,
        out_shape=[jax.ShapeDtypeStruct((b, l, 2 * RWKV_W), F32)] * 3 + [jax.ShapeDtypeStruct((b, l, RWKV_W), F32)] * 2,
        scratch_shapes=[pltpu.VMEM((nb, 1, RWKV_PROJ), F32)],
        compiler_params=_cparams("parallel", "arbitrary"),
        name="rwkv_prep",
    )(proj_rw, shift.reshape(b, 1, RWKV_PROJ), wt['mu'], wt['w0'], wt['w2'], wt['a0'], wt['a2'],
      wt['g2'], wt['kk'], wt['ka'], wt['rk'], wt['head_ones'])


def _wkv_kernel(*refs, tb, ni, paired):
    n_in = 3 if paired else 6
    in_refs, (s0_ref, y_ref, so_ref, s_ref, j_ref, v_ref) = refs[:n_in], refs[n_in:]
    ti = pl.program_id(1)

    @pl.when(ti == 0)
    def _init():
        s_ref[...] = s0_ref[0]

    if paired:
        def spread(t, carry):
            low = lax.broadcasted_iota(jnp.int32, (RWKV_N, LANES), 1) < LANES // 2
            low_v = lax.broadcasted_iota(jnp.int32, (ni, LANES), 1) < LANES // 2
            kkw, cck, rv = (ref[0, t] for ref in in_refs)
            kkw_sw, cck_sw, rv_sw = (pltpu.roll(x, LANES // 2, 1) for x in (kkw, cck, rv))
            j_ref[t, 0], j_ref[t, 1] = jnp.where(low, kkw, kkw_sw), jnp.where(low, kkw_sw, kkw)
            j_ref[t, 2], j_ref[t, 3] = jnp.where(low, cck, cck_sw), jnp.where(low, cck_sw, cck)
            j_ref[t, 4] = jnp.where(low, rv, rv_sw)
            v_ref[t] = jnp.where(low_v, rv_sw[:ni], rv[ni:])
            return carry

        lax.fori_loop(0, tb, spread, 0, unroll=4)

    def step(t, carry):
        if paired:
            kk, w, cc, k, r = (j_ref[t, i] for i in range(5))
            vrow = lambda ii: v_ref[t, ii:ii + 1, :]
        else:
            kk, w, cc, k, r = (ref[0, t] for ref in in_refs[:5])
            vrow = lambda ii: in_refs[5][0, t, ii:ii + 1, :]
        for ii in range(ni):
            s = s_ref[ii]
            sa = -jnp.sum(s * kk, axis=0, keepdims=True)
            sn = s * w + sa * cc + vrow(ii) * k
            s_ref[ii] = sn
            y_ref[0, t, ii:ii + 1, :] = jnp.sum(sn * r, axis=0, keepdims=True)
        return carry

    lax.fori_loop(0, tb, step, 0)

    @pl.when(ti == pl.num_programs(1) - 1)
    def _done():
        so_ref[0] = s_ref[...]


def _wkv_scan(inputs, s0):
    paired = len(inputs) == 3
    g, ni = s0.shape[:2]
    l = inputs[0].shape[1]
    tb = min(64, l)
    assert l % tb == 0 and (not paired or 2 * ni == RWKV_N)
    jblk = pl.BlockSpec((1, tb, RWKV_N, LANES), lambda i, j: (i, j, 0, 0))
    iblk = pl.BlockSpec((1, tb, ni, LANES), lambda i, j: (i, j, 0, 0))
    sblk = pl.BlockSpec((1, ni, RWKV_N, LANES), lambda i, j: (i, 0, 0, 0))
    kern = functools.partial(_wkv_kernel, tb=tb, ni=ni, paired=paired)
    return pl.pallas_call(
        kern,
        grid=(g, l // tb),
        in_specs=([jblk] * 3 if paired else [jblk] * 5 + [iblk]) + [sblk],
        out_specs=[iblk, sblk],
        out_shape=[jax.ShapeDtypeStruct((g, l, ni, LANES), F32),
                   jax.ShapeDtypeStruct((g, ni, RWKV_N, LANES), F32)],
        scratch_shapes=[pltpu.VMEM((ni, RWKV_N, LANES), F32),
                        pltpu.VMEM((tb if paired else 1, 5, RWKV_N, LANES), F32),
                        pltpu.VMEM((tb if paired else 1, ni, LANES), F32)],
        compiler_params=_cparams("parallel", "arbitrary"),
        name="wkv_scan",
    )(*inputs, s0)


def _outproj_kernel(x_ref, oret_ref, y_ref, bonus_ref, gate_ref, lng_ref, lnb_ref, ones_ref,
                    wtop_ref, wbot_ref, o_ref):
    y = y_ref[...]
    yc = y - _head_sums(y, ones_ref[...]) * (1.0 / RWKV_N)
    var = _head_sums(yc * yc, ones_ref[...]) * (1.0 / RWKV_N)
    o = yc * lax.rsqrt(var + RWKV_LN_EPS) * lng_ref[...] + lnb_ref[...]
    o = (o + bonus_ref[...]) * gate_ref[...]
    o_ref[...] = (x_ref[...]
                  + jnp.dot(oret_ref[...].astype(BF16), wtop_ref[...], preferred_element_type=F32)
                  + jnp.dot(o.astype(BF16), wbot_ref[...], preferred_element_type=F32))


def _outproj(x, o_ret, y, bonus, gate, wt):
    t_total = x.shape[0]
    tm = min(TOKEN_TILE, t_total)
    const = lambda shape: pl.BlockSpec(shape, lambda t: (0,) * len(shape))
    half = pl.BlockSpec((tm, RWKV_W), lambda t: (t, 0))
    full = pl.BlockSpec((tm, D_MODEL), lambda t: (t, 0))
    return pl.pallas_call(
        _outproj_kernel,
        grid=(t_total // tm,),
        in_specs=[full, half, half, half, half, const((1, RWKV_W)), const((1, RWKV_W)),
                  const((RWKV_W, RWKV_W)), const((RET_W, D_MODEL)), const((RWKV_W, D_MODEL))],
        out_specs=full,
        out_shape=jax.ShapeDtypeStruct((t_total, D_MODEL), F32),
        compiler_params=_cparams("parallel"),
        name="outproj",
    )(x, o_ret, y, bonus, gate, wt['ln_g'], wt['ln_b'], wt['head_ones'], wt['w_out_top'], wt['w_out_bot'])


def _ple_kernel(h_ref, p_ref, gn_ref, wg_ref, wp_ref, fn_ref, o_ref):
    h = h_ref[...]
    gate = jax.nn.sigmoid(jnp.dot(_rms(h, gn_ref[...]).astype(BF16), wg_ref[...], preferred_element_type=F32))
    h = h + gate * jnp.dot(p_ref[...].astype(BF16), wp_ref[...], preferred_element_type=F32)
    o_ref[...] = _rms(h, fn_ref[...])


def _ple(h, p, wt):
    t_total = h.shape[0]
    tm = min(TOKEN_TILE, t_total)
    const = lambda shape: pl.BlockSpec(shape, lambda t: (0,) * len(shape))
    full = pl.BlockSpec((tm, D_MODEL), lambda t: (t, 0))
    return pl.pallas_call(
        _ple_kernel,
        grid=(t_total // tm,),
        in_specs=[full, pl.BlockSpec((tm, PLE_DIM), lambda t: (t, 0)), const((1, D_MODEL)),
                  const((D_MODEL, D_MODEL)), const((PLE_DIM, D_MODEL)), const((1, D_MODEL))],
        out_specs=full,
        out_shape=jax.ShapeDtypeStruct((t_total, D_MODEL), F32),
        compiler_params=_cparams("parallel"),
        name="ple",
    )(h, p, wt['ple_norm'], wt['ple_gate'], wt['ple_proj'], wt['norm_final'])


def _prep_weights(p):
    row = lambda a: a.reshape(1, -1)
    lane_head = jnp.arange(RWKV_W) // RWKV_N
    return {
        'norm_mix': row(p['norm_mix'][0]),
        'w_ret': p['w_in'][0][:, :RET_PROJ].astype(BF16),
        'w_rw': p['w_in'][0][:, RET_PROJ:].astype(BF16),
        'mu': row(p['rw_mu'][0]), 'w0': row(p['rw_w0'][0]), 'w2': p['rw_w2'][0],
        'a0': row(p['rw_a0'][0]), 'a2': p['rw_a2'][0], 'g2': p['rw_g2'][0],
        'kk': row(p['rw_kk'][0]), 'ka': row(p['rw_ka'][0]), 'rk': row(p['rw_rk'][0]),
        'ln_g': row(p['rw_ln_g'][0]), 'ln_b': row(p['rw_ln_b'][0]),
        'head_ones': (lane_head[:, None] == lane_head[None, :]).astype(BF16),
        'w_out_top': p['w_out'][0][:RET_W].astype(BF16),
        'w_out_bot': p['w_out'][0][RET_W:].astype(BF16),
        'norm_ffn': row(p['norm_ffn'][0]),
        'wqt': jnp.stack(_split(p['peer_wq'][0].T)),
        'keys': jnp.stack(_split(p['peer_keys'][0].reshape(2 * PEER_HEADS, PEER_NKEYS, PEER_DKEY // 2))),
        'u': p['peer_u'][0].astype(BF16),
        'vt': p['peer_v'][0].astype(BF16).T,
        'ple_norm': row(p['ple_norm'][0]),
        'ple_gate': p['ple_gate'][0].astype(BF16),
        'ple_proj': p['ple_proj'][0].astype(BF16),
        'norm_final': row(p['norm_final']),
    }


def _scan_inputs(pairs, b):
    l = pairs[0].shape[1]
    if b * RWKV_HEADS * 2 == LANES:
        return [p.reshape(b, l, 2, RWKV_HEADS, RWKV_N).transpose(1, 4, 2, 0, 3).reshape(1, l, RWKV_N, LANES)
                for p in pairs]
    assert b == LANES
    split = lambda p: [h.reshape(b, l, RWKV_HEADS, RWKV_N).transpose(2, 1, 3, 0)
                       for h in (p[..., :RWKV_W], p[..., RWKV_W:])]
    (kk, w), (cc, k), (r, v) = (split(p) for p in pairs)
    return [kk, w, cc, k, r, v]


def _values_from_lanes(y, b):
    g, l, ni, _ = y.shape
    if g == 1:
        y = y.reshape(l, ni, 2, b, RWKV_HEADS).transpose(3, 0, 4, 2, 1)
    else:
        y = y.transpose(3, 1, 0, 2)
    return y.reshape(b, l, RWKV_W)


def _state_to_lanes(s):
    b = s.shape[0]
    if b * RWKV_HEADS * 2 == LANES:
        s = s.reshape(b, RWKV_HEADS, 2, RWKV_N // 2, RWKV_N).transpose(3, 4, 2, 0, 1)
        return s.reshape(1, RWKV_N // 2, RWKV_N, LANES)
    return s.transpose(1, 2, 3, 0)


def _state_from_lanes(s, b):
    g, ni, _, _ = s.shape
    if g == 1:
        s = s.reshape(ni, RWKV_N, 2, b, RWKV_HEADS).transpose(3, 4, 2, 0, 1)
        return s.reshape(b, RWKV_HEADS, RWKV_N, RWKV_N)
    return s.transpose(3, 0, 1, 2)


def _layer(x, p_l, pos, s_ret, s_wkv, s_shift, wt):
    b, l, _ = x.shape
    x2 = x.reshape(b * l, D_MODEL)
    proj_ret, proj_rw = _inproj(x2, wt['norm_mix'], wt['w_ret'], wt['w_rw'])
    o_ret, s_ret_new = _retention(proj_ret.reshape(b, l, RET_PROJ), s_ret, pos, wt['head_ones'])
    proj_rw = proj_rw.reshape(b, l, RWKV_PROJ)
    new_shift = proj_rw[:, -1]
    kkw, cck, rv, gate, bonus = _rwkv_prep(proj_rw, s_shift, wt)
    y, s_lanes = _wkv_scan(_scan_inputs((kkw, cck, rv), b), _state_to_lanes(s_wkv))
    y = _values_from_lanes(y, b)
    s_wkv_new = _state_from_lanes(s_lanes, b)
    flat = lambda a: a.reshape(b * l, -1)
    h = _outproj(x2, flat(o_ret), flat(y), flat(bonus), flat(gate), wt)
    h = _peer(h, wt['norm_ffn'], wt['wqt'], wt['keys'], wt['u'], wt['vt'])
    y_out = _ple(h, p_l.reshape(b * l, PLE_DIM), wt)
    return y_out.reshape(b, l, D_MODEL), s_ret_new, s_wkv_new, new_shift


def kernel(x_prompt, x_sample, state_ret, state_wkv, state_shift, p_prompt, p_sample, norm_mix, w_in, rw_mu, rw_w0, rw_w2, rw_a0, rw_a2, rw_g2, rw_kk, rw_ka, rw_rk, rw_ln_g, rw_ln_b, w_out, norm_ffn, peer_wq, peer_keys, peer_u, peer_v, ple_norm, ple_gate, ple_proj, norm_final):
    assert norm_mix.shape[0] == 1, "single-layer trunk"
    wt = _prep_weights(dict(
        norm_mix=norm_mix, w_in=w_in, rw_mu=rw_mu, rw_w0=rw_w0, rw_w2=rw_w2, rw_a0=rw_a0, rw_a2=rw_a2,
        rw_g2=rw_g2, rw_kk=rw_kk, rw_ka=rw_ka, rw_rk=rw_rk, rw_ln_g=rw_ln_g, rw_ln_b=rw_ln_b, w_out=w_out,
        norm_ffn=norm_ffn, peer_wq=peer_wq, peer_keys=peer_keys, peer_u=peer_u, peer_v=peer_v,
        ple_norm=ple_norm, ple_gate=ple_gate, ple_proj=ple_proj, norm_final=norm_final))
    bp, lp, _ = x_prompt.shape
    bs, ls, _ = x_sample.shape
    pos_p = jnp.arange(lp, dtype=jnp.int32)
    pos_s = PAST_LEN + jnp.arange(ls, dtype=jnp.int32)
    zero_ret = jnp.zeros((bp, RET_HEADS, RET_DK, RET_DV), F32)
    zero_wkv = jnp.zeros((bp, RWKV_HEADS, RWKV_N, RWKV_N), F32)
    zero_shift = jnp.zeros((bp, RWKV_PROJ), F32)
    yp, rp, wp, sp = _layer(x_prompt, p_prompt[0], pos_p, zero_ret, zero_wkv, zero_shift, wt)
    ys, rs, ws, ss = _layer(x_sample, p_sample[0], pos_s, state_ret[0], state_wkv[0], state_shift[0], wt)
    return (yp, ys, rp[None], wp[None], sp[None], rs[None], ws[None], ss[None])
```

```python
import functools
import math

import jax
import jax.numpy as jnp
from jax import lax
from jax.experimental import pallas as pl
from jax.experimental.pallas import tpu as pltpu

F32 = jnp.float32
BF16 = jnp.bfloat16

D_MODEL = 1024
PAST_LEN = 16384
RET_HEADS = 8
RET_DK = 64
RET_DV = 64
RET_W = RET_HEADS * RET_DV
RWKV_HEADS = 8
RWKV_N = 64
RWKV_W = RWKV_HEADS * RWKV_N
DECAY_LORA = 64
AAA_LORA = 64
GATE_LORA = 128
RET_PROJ = 2 * RET_HEADS * RET_DK + 2 * RET_W
RWKV_PROJ = 3 * RWKV_W + DECAY_LORA + AAA_LORA + GATE_LORA
RET_CHUNK = 128
ROPE_BASE = 10000.0
PEER_HEADS = 8
PEER_NKEYS = 128
PEER_EXPERTS = PEER_NKEYS * PEER_NKEYS
PEER_DKEY = 256
PEER_TOPK = 16
PLE_DIM = 256
RMS_EPS = 1e-6
GN_EPS = 1e-5
RWKV_LN_EPS = 64e-5

LANES = 128
SUBLANES = 8
BF16_ROWS = 16
VMEM_LIMIT = 56 * 1024 * 1024

NEG_INF = float("-inf")


def _cparams(*sem):
    return pltpu.CompilerParams(dimension_semantics=sem, vmem_limit_bytes=VMEM_LIMIT)


def _rms(x, g):
    return x * lax.rsqrt(jnp.mean(x * x, axis=-1, keepdims=True) + RMS_EPS) * g


def _split(x):
    hi = x.astype(BF16)
    return hi, (x - hi.astype(F32)).astype(BF16)


def _dot3(a, b):
    ah, al = _split(a)
    bh, bl = _split(b)
    dot = functools.partial(jnp.dot, preferred_element_type=F32)
    return dot(ah, bh) + dot(ah, bl) + dot(al, bh)


def _head_sums(x, head_ones):
    xh, xl = _split(x)
    dot = functools.partial(jnp.dot, preferred_element_type=F32)
    return dot(xh, head_ones) + dot(xl, head_ones)


def _dot_nt(a, b):
    return lax.dot_general(a, b, (((1,), (1,)), ((), ())), preferred_element_type=F32)


def _dot_tn(a, b):
    return lax.dot_general(a, b, (((0,), (0,)), ((), ())), preferred_element_type=F32)


PEER_TT = 512
PEER_ROWS = 8
PEER_EB = PEER_ROWS * PEER_NKEYS
PEER_CHUNK = 256
PEER_NTOP = PEER_TOPK + 1
_CANDS = tuple((a, b) for a in range(PEER_NTOP) for b in range(PEER_NTOP)
               if (a + 1) * (b + 1) <= PEER_NTOP)


def _sorting_network(n):
    pairs = []
    p = 1
    while p < n:
        k = p
        while k >= 1:
            for j in range(k % p, n - k, 2 * k):
                for i in range(min(k, n - j - k)):
                    if (i + j) // (2 * p) == (i + j + k) // (2 * p):
                        pairs.append((i + j, i + j + k))
            k //= 2
        p *= 2
    return tuple(pairs)


def _top_values(x, n):
    n_tiles = x.shape[0] // SUBLANES
    rows = [x[r * SUBLANES:(r + 1) * SUBLANES] for r in range(n_tiles)]
    for a, b in _sorting_network(n_tiles):
        rows[a], rows[b] = jnp.maximum(rows[a], rows[b]), jnp.minimum(rows[a], rows[b])
    out = []
    for r in range(n):
        m = jnp.max(rows[0], axis=0, keepdims=True)
        out.append(m)
        taken = rows[0] == m
        for q in range(min(n_tiles, n - 1 - r)):
            below = rows[q + 1] if q + 1 < n_tiles else NEG_INF
            rows[q] = jnp.where(taken, below, rows[q])
    return out


def _peer_kernel(h_ref, g_ref, wqt_ref, keys_ref, u_ref, vt_ref, vprev_ref, o_ref,
                 xn_ref, s_ref, e1_ref, bound_ref, rk_ref, e2_ref, top_ref, tau_ref, acc_ref, ht_ref, act_ref,
                 *, tt):
    eb = pl.program_id(1)
    nlb = tt // LANES
    n_ck = PEER_EB // PEER_CHUNK
    chunk = lambda ck: slice(ck * PEER_CHUNK, (ck + 1) * PEER_CHUNK)

    @pl.when(eb == 0)
    def _prepare():
        xn = _rms(h_ref[...], g_ref[...])
        xh, xl = _split(xn)
        xn_ref[...] = (xn * (2.0 ** -0.5)).astype(BF16)
        qt = (_dot_nt(wqt_ref[0], xh) + _dot_nt(wqt_ref[0], xl) + _dot_nt(wqt_ref[1], xh))
        for hc in range(2 * PEER_HEADS):
            qh, ql = _split(qt[hc * PEER_NKEYS:(hc + 1) * PEER_NKEYS, :])
            s_ref[hc] = (jnp.dot(keys_ref[0, hc], qh, preferred_element_type=F32)
                         + jnp.dot(keys_ref[0, hc], ql, preferred_element_type=F32)
                         + jnp.dot(keys_ref[1, hc], qh, preferred_element_type=F32))

        def per_lane_block(lb, carry):
            lanes = pl.ds(pl.multiple_of(lb * LANES, LANES), LANES)
            for hc in range(2 * PEER_HEADS):
                h, c = divmod(hc, 2)
                work = s_ref[hc, :, lanes]
                ex = jnp.exp(work - jnp.max(work, axis=0, keepdims=True))
                if c == 0:
                    e1_ref[h, :, lanes] = ex
                else:
                    for rb in range(PEER_NKEYS // BF16_ROWS):
                        rows = slice(rb * BF16_ROWS, (rb + 1) * BF16_ROWS)
                        e2_ref[lb, rb, h] = ex[rows].astype(BF16)
                for r, m in enumerate(_top_values(work, PEER_NTOP)):
                    top_ref[c, r, pl.ds(h, 1), lanes] = m
            v1 = [top_ref[0, r, :, lanes] for r in range(PEER_NTOP)]
            v2 = [top_ref[1, r, :, lanes] for r in range(PEER_NTOP)]
            cands = [v1[a] + v2[b] for (a, b) in _CANDS]
            cv = []
            for r in range(PEER_TOPK + 1):
                m = functools.reduce(jnp.maximum, cands)
                cv.append(m)
                if r < PEER_TOPK:
                    cands = [jnp.where(x == m, NEG_INF, x) for x in cands]
            z = functools.reduce(lambda a, b: a + b, [jnp.exp(x - cv[0]) for x in cv[:PEER_TOPK]])
            tau = 0.5 * (cv[PEER_TOPK - 1] + cv[PEER_TOPK])
            tau_ref[1, :, lanes] = (2.0 ** -0.5) / z
            for h in range(PEER_HEADS):
                theta = tau[h:h + 1] - s_ref[2 * h, :, lanes]
                s2 = s_ref[2 * h + 1, :, lanes]
                bound = jnp.zeros_like(theta)
                rank = jnp.zeros_like(s2)
                for r in range(PEER_NTOP):
                    v2r = top_ref[1, r, h:h + 1, lanes]
                    bound = bound + jnp.where(v2r >= theta, 1.0, 0.0)
                    rank = rank + jnp.where(s2 < v2r, 1.0, 0.0)
                bound_ref[h, :, lanes] = bound
                for rb in range(PEER_NKEYS // BF16_ROWS):
                    rows = slice(rb * BF16_ROWS, (rb + 1) * BF16_ROWS)
                    rk_ref[lb, rb, h] = rank[rows].astype(BF16)
            return carry

        lax.fori_loop(0, nlb, per_lane_block, 0)
        acc_ref[...] = jnp.zeros_like(acc_ref)
        act_ref[PEER_EB // 2:, :] = jnp.zeros((PEER_EB // 2, tt), BF16)

    irows =pl.ds(pl.multiple_of(eb * PEER_ROWS, PEER_ROWS), PEER_ROWS)
    shape = (BF16_ROWS, LANES)

    def gate_rows(il, lb):
        lanes = slice(lb * LANES, (lb + 1) * LANES)
        rz = tau_ref[1, :, lanes]
        bound, e1 = [], []
        for h in range(PEER_HEADS):
            bound.append(jnp.broadcast_to(bound_ref[h, irows, lanes][il:il + 1], shape).astype(BF16))
            e1.append(jnp.broadcast_to(e1_ref[h, irows, lanes][il:il + 1] * rz[h:h + 1], shape).astype(BF16))
        for jb in range(PEER_NKEYS // BF16_ROWS):
            terms = [jnp.where(rk_ref[lb, jb, h] < bound[h], e2_ref[lb, jb, h], 0.0) * e1[h]
                     for h in range(PEER_HEADS)]
            while len(terms) > 1:
                terms = [terms[i] + terms[i + 1] for i in range(0, len(terms), 2)]
            r0 = il * PEER_NKEYS + jb * BF16_ROWS
            hv = ht_ref[r0:r0 + BF16_ROWS, lanes]
            act_ref[r0:r0 + BF16_ROWS, lanes] = (hv * (1.0 + lax.erf(hv))).astype(BF16) * terms[0]

    lo, hi = slice(0, PEER_EB // 2), slice(PEER_EB // 2, PEER_EB)

    def hval(ck):
        ht_ref[chunk(ck), :] = _dot_nt(u_ref[chunk(ck), :], xn_ref[...])

    def gates(cks):
        for ck in cks:
            for il in range(ck * PEER_CHUNK // PEER_NKEYS, (ck + 1) * PEER_CHUNK // PEER_NKEYS):
                for lb in range(nlb):
                    gate_rows(il, lb)

    first, second = range(n_ck // 2), range(n_ck // 2, n_ck)
    for ck in first:
        hval(ck)
    acc_ref[...] += jnp.dot(vprev_ref[...], act_ref[hi, :], preferred_element_type=F32)
    for ck in second:
        hval(ck)
    gates(first)
    acc_ref[...] += jnp.dot(vt_ref[:, lo], act_ref[lo, :], preferred_element_type=F32)
    gates(second)

    @pl.when(eb == pl.num_programs(1) - 1)
    def _finish():
        tail = jnp.dot(vt_ref[:, hi], act_ref[hi, :], preferred_element_type=F32)
        o_ref[...] = h_ref[...] + (acc_ref[...] + tail).T


def _peer(h, norm_g, wqt, keys, u, vt):
    t_total = h.shape[0]
    tt = min(PEER_TT, t_total)
    assert t_total % tt == 0 and tt % LANES == 0
    n_eb = PEER_EXPERTS // PEER_EB
    n_ck = PEER_EB // PEER_CHUNK
    kern = functools.partial(_peer_kernel, tt=tt)
    return pl.pallas_call(
        kern,
        grid=(t_total // tt, n_eb),
        in_specs=[
            pl.BlockSpec((tt, D_MODEL), lambda t, e: (t, 0)),
            pl.BlockSpec((1, D_MODEL), lambda t, e: (0, 0)),
            pl.BlockSpec((2, PEER_HEADS * PEER_DKEY, D_MODEL), lambda t, e: (0, 0, 0)),
            pl.BlockSpec((2, 2 * PEER_HEADS, PEER_NKEYS, PEER_DKEY // 2), lambda t, e: (0, 0, 0, 0)),
            pl.BlockSpec((PEER_EB, D_MODEL), lambda t, e: (e, 0)),
            pl.BlockSpec((D_MODEL, PEER_EB), lambda t, e: (0, e)),
            pl.BlockSpec((D_MODEL, PEER_EB // 2), lambda t, e: (0, jnp.maximum(2 * e - 1, 0))),
        ],
        out_specs=pl.BlockSpec((tt, D_MODEL), lambda t, e: (t, 0)),
        out_shape=jax.ShapeDtypeStruct((t_total, D_MODEL), F32),
        scratch_shapes=[
            pltpu.VMEM((tt, D_MODEL), BF16),
            pltpu.VMEM((2 * PEER_HEADS, PEER_NKEYS, tt), F32),
            pltpu.VMEM((PEER_HEADS, PEER_NKEYS, tt), F32),
            pltpu.VMEM((PEER_HEADS, PEER_NKEYS, tt), F32),
            pltpu.VMEM((tt // LANES, PEER_NKEYS // BF16_ROWS, PEER_HEADS, BF16_ROWS, LANES), BF16),
            pltpu.VMEM((tt // LANES, PEER_NKEYS // BF16_ROWS, PEER_HEADS, BF16_ROWS, LANES), BF16),
            pltpu.VMEM((2, PEER_NTOP, PEER_HEADS, tt), F32),
            pltpu.VMEM((2, PEER_HEADS, tt), F32),
            pltpu.VMEM((D_MODEL, tt), F32),
            pltpu.VMEM((PEER_EB, tt), F32),
            pltpu.VMEM((PEER_EB, tt), BF16),
        ],
        compiler_params=_cparams("parallel", "arbitrary"),
        name="peer",
    )(h, norm_g, wqt, keys, u, vt, vt)


TOKEN_TILE = 512


def _inproj_kernel(x_ref, g_ref, wr_ref, ww_ref, pr_ref, pw_ref):
    xn = _rms(x_ref[...], g_ref[...]).astype(BF16)
    pr_ref[...] = jnp.dot(xn, wr_ref[...], preferred_element_type=F32)
    pw_ref[...] = jnp.dot(xn, ww_ref[...], preferred_element_type=F32)


def _inproj(x, norm_g, w_ret, w_rw):
    t_total = x.shape[0]
    tm = min(TOKEN_TILE, t_total)
    assert t_total % tm == 0
    const = lambda shape: pl.BlockSpec(shape, lambda t: (0,) * len(shape))
    return pl.pallas_call(
        _inproj_kernel,
        grid=(t_total // tm,),
        in_specs=[pl.BlockSpec((tm, D_MODEL), lambda t: (t, 0)), const((1, D_MODEL)),
                  const((D_MODEL, RET_PROJ)), const((D_MODEL, RWKV_PROJ))],
        out_specs=[pl.BlockSpec((tm, RET_PROJ), lambda t: (t, 0)),
                   pl.BlockSpec((tm, RWKV_PROJ), lambda t: (t, 0))],
        out_shape=[jax.ShapeDtypeStruct((t_total, RET_PROJ), F32),
                   jax.ShapeDtypeStruct((t_total, RWKV_PROJ), F32)],
        compiler_params=_cparams("parallel"),
        name="inproj",
    )(x, norm_g, w_ret, w_rw)


def _rope_partner(x):
    lane = lax.broadcasted_iota(jnp.int32, x.shape, 1)
    up = pltpu.roll(x, x.shape[1] - RET_DK // 2, 1)
    down = pltpu.roll(x, RET_DK // 2, 1)
    return jnp.where(lane % RET_DK < RET_DK // 2, up, down)


def _retention_kernel(p_ref, s0_ref, cos_ref, sin_ref, dmat_ref, xz_ref, ones_ref, o_ref, so_ref, s_ref,
                      *, nb, g_chunk):
    ci = pl.program_id(1)
    qk = RET_HEADS * RET_DK
    n_pairs = RET_HEADS // 2
    row_head = lax.broadcasted_iota(jnp.int32, (LANES, LANES), 0) // RET_DK
    lane_head = lax.broadcasted_iota(jnp.int32, (LANES, LANES), 1) // RET_DV
    same_head = row_head == lane_head

    @pl.when(ci == 0)
    def _init():
        s_ref[...] = jnp.zeros_like(s_ref)
        for p in range(n_pairs):
            s_ref[:, p, :RET_DK, :RET_DV] = s0_ref[:, 2 * p]
            s_ref[:, p, RET_DK:, RET_DV:] = s0_ref[:, 2 * p + 1]

    cos = cos_ref[...]
    sin = sin_ref[...]

    def per_seq(n, carry):
        x = p_ref[n]
        q = x[:, :qk]
        k = x[:, qk:2 * qk]
        q = q * cos + _rope_partner(q) * sin
        k = (k * cos + _rope_partner(k) * sin) * (RET_DK ** -0.5)
        lane_is_hi = lax.broadcasted_iota(jnp.int32, (q.shape[0], LANES), 1) >= RET_DK
        for p in range(n_pairs):
            ls = slice(p * LANES, (p + 1) * LANES)
            qp, kp, vp = q[:, ls], k[:, ls], x[:, 2 * qk + p * LANES:2 * qk + (p + 1) * LANES]
            kb, vb = kp.astype(BF16), vp.astype(BF16)
            s_p = s_ref[n, p]
            o = jnp.dot(qp.astype(BF16), s_p.astype(BF16), preferred_element_type=F32) * xz_ref[0, p]
            for hh in range(2):
                mine = lane_is_hi if hh else jnp.logical_not(lane_is_hi)
                inner = _dot_nt(jnp.where(mine, qp, 0.0).astype(BF16), kb) * dmat_ref[2 * p + hh]
                o = o + jnp.dot(inner.astype(BF16), jnp.where(mine, vp, 0.0).astype(BF16),
                                preferred_element_type=F32)
            o_ref[n, :, ls] = o
            decay = jnp.where(row_head == 0, g_chunk[2 * p], g_chunk[2 * p + 1])
            grown = _dot_tn((kp * xz_ref[1, p]).astype(BF16), vb)
            s_ref[n, p] = s_p * decay + jnp.where(same_head, grown, 0.0)
        o = o_ref[n]
        g = x[:, 2 * qk + RET_W:]
        oc = o - _head_sums(o, ones_ref[...]) * (1.0 / RET_DV)
        var = _head_sums(oc * oc, ones_ref[...]) * (1.0 / RET_DV)
        o_ref[n] = oc * lax.rsqrt(var + GN_EPS) * (g * jax.nn.sigmoid(g))
        return carry

    lax.fori_loop(0, nb, per_seq, 0, unroll=2)

    @pl.when(ci == pl.num_programs(1) - 1)
    def _done():
        for p in range(n_pairs):
            so_ref[:, 2 * p] = s_ref[:, p, :RET_DK, :RET_DV]
            so_ref[:, 2 * p + 1] = s_ref[:, p, RET_DK:, RET_DV:]


def _retention(proj_ret, s0, pos, head_ones):
    assert RET_W == RWKV_W and RET_DV == RWKV_N
    b, l, _ = proj_ret.shape
    c = RET_CHUNK if l % RET_CHUNK == 0 else l
    nb = 2 if c == RET_CHUNK else 16
    assert b % nb == 0
    half = RET_DK // 2
    inv = ROPE_BASE ** (-jnp.arange(half, dtype=F32) / half)
    ang = pos.astype(F32)[:, None] * inv[None, :]
    cos = jnp.tile(jnp.cos(ang), (1, 2 * RET_HEADS))
    sin = jnp.tile(jnp.concatenate([-jnp.sin(ang), jnp.sin(ang)], axis=1), (1, RET_HEADS))
    lg = jnp.log(1.0 - 2.0 ** (-5.0 - jnp.arange(RET_HEADS, dtype=F32)))
    idx = jnp.arange(c, dtype=F32)
    diff = idx[:, None] - idx[None, :]
    dmat = jnp.where(diff[None] >= 0, jnp.exp(jnp.maximum(diff, 0.0)[None] * lg[:, None, None]), 0.0)
    xi = jnp.exp((idx + 1.0)[None, :] * lg[:, None])
    zeta = jnp.exp((c - 1.0 - idx)[None, :] * lg[:, None])
    pair_lanes = lambda a: jnp.repeat(a.reshape(RET_HEADS // 2, 2, c).transpose(0, 2, 1), RET_DV, axis=-1)
    xz = jnp.stack([pair_lanes(xi), pair_lanes(zeta)])
    g_chunk = tuple(math.exp(c * math.log(1.0 - 2.0 ** (-5.0 - h))) for h in range(RET_HEADS))
    kern = functools.partial(_retention_kernel, nb=nb, g_chunk=g_chunk)
    return pl.pallas_call(
        kern,
        grid=(b // nb, l // c),
        in_specs=[
            pl.BlockSpec((nb, c, RET_PROJ), lambda i, j: (i, j, 0)),
            pl.BlockSpec((nb, RET_HEADS, RET_DK, RET_DV), lambda i, j: (i, 0, 0, 0)),
            pl.BlockSpec((c, RET_HEADS * RET_DK), lambda i, j: (j, 0)),
            pl.BlockSpec((c, RET_HEADS * RET_DK), lambda i, j: (j, 0)),
            pl.BlockSpec((RET_HEADS, c, c), lambda i, j: (0, 0, 0)),
            pl.BlockSpec((2, RET_HEADS // 2, c, LANES), lambda i, j: (0, 0, 0, 0)),
            pl.BlockSpec((RET_W, RET_W), lambda i, j: (0, 0)),
        ],
        out_specs=[pl.BlockSpec((nb, c, RET_W), lambda i, j: (i, j, 0)),
                   pl.BlockSpec((nb, RET_HEADS, RET_DK, RET_DV), lambda i, j: (i, 0, 0, 0))],
        out_shape=[jax.ShapeDtypeStruct((b, l, RET_W), F32),
                   jax.ShapeDtypeStruct((b, RET_HEADS, RET_DK, RET_DV), F32)],
        scratch_shapes=[pltpu.VMEM((nb, RET_HEADS // 2, 2 * RET_DK, 2 * RET_DV), F32)],
        compiler_params=_cparams("parallel", "arbitrary"),
        name="retention",
    )(proj_ret, s0, cos, sin, dmat, xz, head_ones)


def _softplus(x):
    return jnp.maximum(x, 0.0) + jnp.log1p(jnp.exp(-jnp.abs(x)))


def _rwkv_prep_kernel(x_ref, sh_ref, mu_ref, w0_ref, w2_ref, a0_ref, a2_ref, g2_ref, kks_ref, ka_ref,
                      rk_ref, ones_ref, kkw_ref, cck_ref, rv_ref, gate_ref, bonus_ref,
                      carry_ref, *, nb, tb):
    ti = pl.program_id(1)
    x = x_ref[...]
    first = jnp.where(ti == 0, sh_ref[...], carry_ref[...])
    prev = pltpu.roll(x.reshape(nb * tb, RWKV_PROJ), 1, 0).reshape(nb, tb, RWKV_PROJ)
    tpos = lax.broadcasted_iota(jnp.int32, x.shape, 1)
    prev = jnp.where(tpos == 0, first, prev)
    carry_ref[...] = x[:, tb - 1:tb, :]
    f = (x + (prev - x) * mu_ref[...]).reshape(nb * tb, RWKV_PROJ)
    o1, o2, o3 = RWKV_W, 2 * RWKV_W, 3 * RWKV_W
    r, k7, v7 = f[:, :o1], f[:, o1:o2], f[:, o2:o3]
    fw = f[:, o3:o3 + DECAY_LORA]
    fa = f[:, o3 + DECAY_LORA:o3 + DECAY_LORA + AAA_LORA]
    fg = f[:, o3 + DECAY_LORA + AAA_LORA:]
    w = -_softplus(-(w0_ref[...] + _dot3(jnp.tanh(fw), w2_ref[...]))) - 0.5
    a = jax.nn.sigmoid(a0_ref[...] + _dot3(fa, a2_ref[...]))
    gate = _dot3(jax.nn.sigmoid(fg), g2_ref[...])
    kk = k7 * kks_ref[...]
    kk = kk / jnp.maximum(jnp.sqrt(_head_sums(kk * kk, ones_ref[...])), 1e-12)
    k = k7 * (1.0 + (a - 1.0) * ka_ref[...])
    out3 = lambda y: y.reshape(nb, tb, RWKV_W)
    kkw_ref[:, :, :RWKV_W] = out3(kk)
    kkw_ref[:, :, RWKV_W:] = out3(jnp.exp(-jnp.exp(w)))
    cck_ref[:, :, :RWKV_W] = out3(kk * a)
    cck_ref[:, :, RWKV_W:] = out3(k)
    rv_ref[:, :, :RWKV_W] = out3(r)
    rv_ref[:, :, RWKV_W:] = out3(v7)
    gate_ref[...] = out3(gate)
    bonus_ref[...] = out3(_head_sums(r * k * rk_ref[...], ones_ref[...]) * v7)


def _rwkv_prep(proj_rw, shift, wt):
    b, l, _ = proj_rw.shape
    tb = min(TOKEN_TILE, l)
    nb = max(1, 128 // tb)
    assert l % tb == 0 and b % nb == 0 and tb % SUBLANES == 0
    const = lambda shape: pl.BlockSpec(shape, lambda i, j: (0,) * len(shape))
    blk = pl.BlockSpec((nb, tb, RWKV_W), lambda i, j: (i, j, 0))
    blk2 = pl.BlockSpec((nb, tb, 2 * RWKV_W), lambda i, j: (i, j, 0))
    kern = functools.partial(_rwkv_prep_kernel, nb=nb, tb=tb)
    return pl.pallas_call(
        kern,
        grid=(b // nb, l // tb),
        in_specs=[
            pl.BlockSpec((nb, tb, RWKV_PROJ), lambda i, j: (i, j, 0)),
            pl.BlockSpec((nb, 1, RWKV_PROJ), lambda i, j: (i, 0, 0)),
            const((1, RWKV_PROJ)), const((1, RWKV_W)), const((DECAY_LORA, RWKV_W)),
            const((1, RWKV_W)), const((AAA_LORA, RWKV_W)), const((GATE_LORA, RWKV_W)),
            const((1, RWKV_W)), const((1, RWKV_W)), const((1, RWKV_W)), const((RWKV_W, RWKV_W)),
        ],
        out_specs=[blk2] * 3 + [blk] * 2,
        out_shape=[jax.ShapeDtypeStruct((b, l, 2 * RWKV_W), F32)] * 3 + [jax.ShapeDtypeStruct((b, l, RWKV_W), F32)] * 2,
        scratch_shapes=[pltpu.VMEM((nb, 1, RWKV_PROJ), F32)],
        compiler_params=_cparams("parallel", "arbitrary"),
        name="rwkv_prep",
    )(proj_rw, shift.reshape(b, 1, RWKV_PROJ), wt['mu'], wt['w0'], wt['w2'], wt['a0'], wt['a2'],
      wt['g2'], wt['kk'], wt['ka'], wt['rk'], wt['head_ones'])


def _wkv_kernel(*refs, tb, ni, paired):
    n_in = 3 if paired else 6
    in_refs, (s0_ref, y_ref, so_ref, s_ref, j_ref, v_ref) = refs[:n_in], refs[n_in:]
    ti = pl.program_id(1)

    @pl.when(ti == 0)
    def _init():
        s_ref[...] = s0_ref[0]

    if paired:
        def spread(t, carry):
            low = lax.broadcasted_iota(jnp.int32, (RWKV_N, LANES), 1) < LANES // 2
            low_v = lax.broadcasted_iota(jnp.int32, (ni, LANES), 1) < LANES // 2
            kkw, cck, rv = (ref[0, t] for ref in in_refs)
            kkw_sw, cck_sw, rv_sw = (pltpu.roll(x, LANES // 2, 1) for x in (kkw, cck, rv))
            j_ref[t, 0], j_ref[t, 1] = jnp.where(low, kkw, kkw_sw), jnp.where(low, kkw_sw, kkw)
            j_ref[t, 2], j_ref[t, 3] = jnp.where(low, cck, cck_sw), jnp.where(low, cck_sw, cck)
            j_ref[t, 4] = jnp.where(low, rv, rv_sw)
            v_ref[t] = jnp.where(low_v, rv_sw[:ni], rv[ni:])
            return carry

        lax.fori_loop(0, tb, spread, 0, unroll=8)

    def step(t, carry):
        if paired:
            kk, w, cc, k, r = (j_ref[t, i] for i in range(5))
            vrow = lambda ii: v_ref[t, ii:ii + 1, :]
        else:
            kk, w, cc, k, r = (ref[0, t] for ref in in_refs[:5])
            vrow = lambda ii: in_refs[5][0, t, ii:ii + 1, :]
        for ii in range(ni):
            s = s_ref[ii]
            sa = -jnp.sum(s * kk, axis=0, keepdims=True)
            sn = s * w + sa * cc + vrow(ii) * k
            s_ref[ii] = sn
            y_ref[0, t, ii:ii + 1, :] = jnp.sum(sn * r, axis=0, keepdims=True)
        return carry

    lax.fori_loop(0, tb, step, 0)

    @pl.when(ti == pl.num_programs(1) - 1)
    def _done():
        so_ref[0] = s_ref[...]


def _wkv_scan(inputs, s0):
    paired = len(inputs) == 3
    g, ni = s0.shape[:2]
    l = inputs[0].shape[1]
    tb = min(64, l)
    assert l % tb == 0 and (not paired or 2 * ni == RWKV_N)
    jblk = pl.BlockSpec((1, tb, RWKV_N, LANES), lambda i, j: (i, j, 0, 0))
    iblk = pl.BlockSpec((1, tb, ni, LANES), lambda i, j: (i, j, 0, 0))
    sblk = pl.BlockSpec((1, ni, RWKV_N, LANES), lambda i, j: (i, 0, 0, 0))
    kern = functools.partial(_wkv_kernel, tb=tb, ni=ni, paired=paired)
    return pl.pallas_call(
        kern,
        grid=(g, l // tb),
        in_specs=([jblk] * 3 if paired else [jblk] * 5 + [iblk]) + [sblk],
        out_specs=[iblk, sblk],
        out_shape=[jax.ShapeDtypeStruct((g, l, ni, LANES), F32),
                   jax.ShapeDtypeStruct((g, ni, RWKV_N, LANES), F32)],
        scratch_shapes=[pltpu.VMEM((ni, RWKV_N, LANES), F32),
                        pltpu.VMEM((tb if paired else 1, 5, RWKV_N, LANES), F32),
                        pltpu.VMEM((tb if paired else 1, ni, LANES), F32)],
        compiler_params=_cparams("parallel", "arbitrary"),
        name="wkv_scan",
    )(*inputs, s0)


def _outproj_kernel(x_ref, oret_ref, y_ref, bonus_ref, gate_ref, lng_ref, lnb_ref, ones_ref,
                    wtop_ref, wbot_ref, o_ref):
    y = y_ref[...]
    yc = y - _head_sums(y, ones_ref[...]) * (1.0 / RWKV_N)
    var = _head_sums(yc * yc, ones_ref[...]) * (1.0 / RWKV_N)
    o = yc * lax.rsqrt(var + RWKV_LN_EPS) * lng_ref[...] + lnb_ref[...]
    o = (o + bonus_ref[...]) * gate_ref[...]
    o_ref[...] = (x_ref[...]
                  + jnp.dot(oret_ref[...].astype(BF16), wtop_ref[...], preferred_element_type=F32)
                  + jnp.dot(o.astype(BF16), wbot_ref[...], preferred_element_type=F32))


def _outproj(x, o_ret, y, bonus, gate, wt):
    t_total = x.shape[0]
    tm = min(TOKEN_TILE, t_total)
    const = lambda shape: pl.BlockSpec(shape, lambda t: (0,) * len(shape))
    half = pl.BlockSpec((tm, RWKV_W), lambda t: (t, 0))
    full = pl.BlockSpec((tm, D_MODEL), lambda t: (t, 0))
    return pl.pallas_call(
        _outproj_kernel,
        grid=(t_total // tm,),
        in_specs=[full, half, half, half, half, const((1, RWKV_W)), const((1, RWKV_W)),
                  const((RWKV_W, RWKV_W)), const((RET_W, D_MODEL)), const((RWKV_W, D_MODEL))],
        out_specs=full,
        out_shape=jax.ShapeDtypeStruct((t_total, D_MODEL), F32),
        compiler_params=_cparams("parallel"),
        name="outproj",
    )(x, o_ret, y, bonus, gate, wt['ln_g'], wt['ln_b'], wt['head_ones'], wt['w_out_top'], wt['w_out_bot'])


def _ple_kernel(h_ref, p_ref, gn_ref, wg_ref, wp_ref, fn_ref, o_ref):
    h = h_ref[...]
    gate = jax.nn.sigmoid(jnp.dot(_rms(h, gn_ref[...]).astype(BF16), wg_ref[...], preferred_element_type=F32))
    h = h + gate * jnp.dot(p_ref[...].astype(BF16), wp_ref[...], preferred_element_type=F32)
    o_ref[...] = _rms(h, fn_ref[...])


def _ple(h, p, wt):
    t_total = h.shape[0]
    tm = min(TOKEN_TILE, t_total)
    const = lambda shape: pl.BlockSpec(shape, lambda t: (0,) * len(shape))
    full = pl.BlockSpec((tm, D_MODEL), lambda t: (t, 0))
    return pl.pallas_call(
        _ple_kernel,
        grid=(t_total // tm,),
        in_specs=[full, pl.BlockSpec((tm, PLE_DIM), lambda t: (t, 0)), const((1, D_MODEL)),
                  const((D_MODEL, D_MODEL)), const((PLE_DIM, D_MODEL)), const((1, D_MODEL))],
        out_specs=full,
        out_shape=jax.ShapeDtypeStruct((t_total, D_MODEL), F32),
        compiler_params=_cparams("parallel"),
        name="ple",
    )(h, p, wt['ple_norm'], wt['ple_gate'], wt['ple_proj'], wt['norm_final'])


def _prep_weights(p):
    row = lambda a: a.reshape(1, -1)
    lane_head = jnp.arange(RWKV_W) // RWKV_N
    return {
        'norm_mix': row(p['norm_mix'][0]),
        'w_ret': p['w_in'][0][:, :RET_PROJ].astype(BF16),
        'w_rw': p['w_in'][0][:, RET_PROJ:].astype(BF16),
        'mu': row(p['rw_mu'][0]), 'w0': row(p['rw_w0'][0]), 'w2': p['rw_w2'][0],
        'a0': row(p['rw_a0'][0]), 'a2': p['rw_a2'][0], 'g2': p['rw_g2'][0],
        'kk': row(p['rw_kk'][0]), 'ka': row(p['rw_ka'][0]), 'rk': row(p['rw_rk'][0]),
        'ln_g': row(p['rw_ln_g'][0]), 'ln_b': row(p['rw_ln_b'][0]),
        'head_ones': (lane_head[:, None] == lane_head[None, :]).astype(BF16),
        'w_out_top': p['w_out'][0][:RET_W].astype(BF16),
        'w_out_bot': p['w_out'][0][RET_W:].astype(BF16),
        'norm_ffn': row(p['norm_ffn'][0]),
        'wqt': jnp.stack(_split(p['peer_wq'][0].T)),
        'keys': jnp.stack(_split(p['peer_keys'][0].reshape(2 * PEER_HEADS, PEER_NKEYS, PEER_DKEY // 2))),
        'u': p['peer_u'][0].astype(BF16),
        'vt': p['peer_v'][0].astype(BF16).T,
        'ple_norm': row(p['ple_norm'][0]),
        'ple_gate': p['ple_gate'][0].astype(BF16),
        'ple_proj': p['ple_proj'][0].astype(BF16),
        'norm_final': row(p['norm_final']),
    }


def _scan_inputs(pairs, b):
    l = pairs[0].shape[1]
    if b * RWKV_HEADS * 2 == LANES:
        return [p.reshape(b, l, 2, RWKV_HEADS, RWKV_N).transpose(1, 4, 2, 0, 3).reshape(1, l, RWKV_N, LANES)
                for p in pairs]
    assert b == LANES
    split = lambda p: [h.reshape(b, l, RWKV_HEADS, RWKV_N).transpose(2, 1, 3, 0)
                       for h in (p[..., :RWKV_W], p[..., RWKV_W:])]
    (kk, w), (cc, k), (r, v) = (split(p) for p in pairs)
    return [kk, w, cc, k, r, v]


def _values_from_lanes(y, b):
    g, l, ni, _ = y.shape
    if g == 1:
        y = y.reshape(l, ni, 2, b, RWKV_HEADS).transpose(3, 0, 4, 2, 1)
    else:
        y = y.transpose(3, 1, 0, 2)
    return y.reshape(b, l, RWKV_W)


def _state_to_lanes(s):
    b = s.shape[0]
    if b * RWKV_HEADS * 2 == LANES:
        s = s.reshape(b, RWKV_HEADS, 2, RWKV_N // 2, RWKV_N).transpose(3, 4, 2, 0, 1)
        return s.reshape(1, RWKV_N // 2, RWKV_N, LANES)
    return s.transpose(1, 2, 3, 0)


def _state_from_lanes(s, b):
    g, ni, _, _ = s.shape
    if g == 1:
        s = s.reshape(ni, RWKV_N, 2, b, RWKV_HEADS).transpose(3, 4, 2, 0, 1)
        return s.reshape(b, RWKV_HEADS, RWKV_N, RWKV_N)
    return s.transpose(3, 0, 1, 2)


def _layer(x, p_l, pos, s_ret, s_wkv, s_shift, wt):
    b, l, _ = x.shape
    x2 = x.reshape(b * l, D_MODEL)
    proj_ret, proj_rw = _inproj(x2, wt['norm_mix'], wt['w_ret'], wt['w_rw'])
    o_ret, s_ret_new = _retention(proj_ret.reshape(b, l, RET_PROJ), s_ret, pos, wt['head_ones'])
    proj_rw = proj_rw.reshape(b, l, RWKV_PROJ)
    new_shift = proj_rw[:, -1]
    kkw, cck, rv, gate, bonus = _rwkv_prep(proj_rw, s_shift, wt)
    y, s_lanes = _wkv_scan(_scan_inputs((kkw, cck, rv), b), _state_to_lanes(s_wkv))
    y = _values_from_lanes(y, b)
    s_wkv_new = _state_from_lanes(s_lanes, b)
    flat = lambda a: a.reshape(b * l, -1)
    h = _outproj(x2, flat(o_ret), flat(y), flat(bonus), flat(gate), wt)
    h = _peer(h, wt['norm_ffn'], wt['wqt'], wt['keys'], wt['u'], wt['vt'])
    y_out = _ple(h, p_l.reshape(b * l, PLE_DIM), wt)
    return y_out.reshape(b, l, D_MODEL), s_ret_new, s_wkv_new, new_shift


def kernel(x_prompt, x_sample, state_ret, state_wkv, state_shift, p_prompt, p_sample, norm_mix, w_in, rw_mu, rw_w0, rw_w2, rw_a0, rw_a2, rw_g2, rw_kk, rw_ka, rw_rk, rw_ln_g, rw_ln_b, w_out, norm_ffn, peer_wq, peer_keys, peer_u, peer_v, ple_norm, ple_gate, ple_proj, norm_final):
    assert norm_mix.shape[0] == 1, "single-layer trunk"
    wt = _prep_weights(dict(
        norm_mix=norm_mix, w_in=w_in, rw_mu=rw_mu, rw_w0=rw_w0, rw_w2=rw_w2, rw_a0=rw_a0, rw_a2=rw_a2,
        rw_g2=rw_g2, rw_kk=rw_kk, rw_ka=rw_ka, rw_rk=rw_rk, rw_ln_g=rw_ln_g, rw_ln_b=rw_ln_b, w_out=w_out,
        norm_ffn=norm_ffn, peer_wq=peer_wq, peer_keys=peer_keys, peer_u=peer_u, peer_v=peer_v,
        ple_norm=ple_norm, ple_gate=ple_gate, ple_proj=ple_proj, norm_final=norm_final))
    bp, lp, _ = x_prompt.shape
    bs, ls, _ = x_sample.shape
    pos_p = jnp.arange(lp, dtype=jnp.int32)
    pos_s = PAST_LEN + jnp.arange(ls, dtype=jnp.int32)
    zero_ret = jnp.zeros((bp, RET_HEADS, RET_DK, RET_DV), F32)
    zero_wkv = jnp.zeros((bp, RWKV_HEADS, RWKV_N, RWKV_N), F32)
    zero_shift = jnp.zeros((bp, RWKV_PROJ), F32)
    yp, rp, wp, sp = _layer(x_prompt, p_prompt[0], pos_p, zero_ret, zero_wkv, zero_shift, wt)
    ys, rs, ws, ss = _layer(x_sample, p_sample[0], pos_s, state_ret[0], state_wkv[0], state_shift[0], wt)
    return (yp, ys, rp[None], wp[None], sp[None], rs[None], ws[None], ss[None])
```

```python
import functools
import math

import jax
import jax.numpy as jnp
from jax import lax
from jax.experimental import pallas as pl
from jax.experimental.pallas import tpu as pltpu

F32 = jnp.float32
BF16 = jnp.bfloat16

D_MODEL = 1024
PAST_LEN = 16384
RET_HEADS = 8
RET_DK = 64
RET_DV = 64
RET_W = RET_HEADS * RET_DV
RWKV_HEADS = 8
RWKV_N = 64
RWKV_W = RWKV_HEADS * RWKV_N
DECAY_LORA = 64
AAA_LORA = 64
GATE_LORA = 128
RET_PROJ = 2 * RET_HEADS * RET_DK + 2 * RET_W
RWKV_PROJ = 3 * RWKV_W + DECAY_LORA + AAA_LORA + GATE_LORA
RET_CHUNK = 128
ROPE_BASE = 10000.0
PEER_HEADS = 8
PEER_NKEYS = 128
PEER_EXPERTS = PEER_NKEYS * PEER_NKEYS
PEER_DKEY = 256
PEER_TOPK = 16
PLE_DIM = 256
RMS_EPS = 1e-6
GN_EPS = 1e-5
RWKV_LN_EPS = 64e-5

LANES = 128
SUBLANES = 8
BF16_ROWS = 16
VMEM_LIMIT = 56 * 1024 * 1024

NEG_INF = float("-inf")


def _cparams(*sem):
    return pltpu.CompilerParams(dimension_semantics=sem, vmem_limit_bytes=VMEM_LIMIT)


def _rms(x, g):
    return x * lax.rsqrt(jnp.mean(x * x, axis=-1, keepdims=True) + RMS_EPS) * g


def _split(x):
    hi = x.astype(BF16)
    return hi, (x - hi.astype(F32)).astype(BF16)


def _dot3(a, b):
    ah, al = _split(a)
    bh, bl = _split(b)
    dot = functools.partial(jnp.dot, preferred_element_type=F32)
    return dot(ah, bh) + dot(ah, bl) + dot(al, bh)


def _head_sums(x, head_ones):
    xh, xl = _split(x)
    dot = functools.partial(jnp.dot, preferred_element_type=F32)
    return dot(xh, head_ones) + dot(xl, head_ones)


def _dot_nt(a, b):
    return lax.dot_general(a, b, (((1,), (1,)), ((), ())), preferred_element_type=F32)


def _dot_tn(a, b):
    return lax.dot_general(a, b, (((0,), (0,)), ((), ())), preferred_element_type=F32)


PEER_TT = 512
PEER_ROWS = 8
PEER_EB = PEER_ROWS * PEER_NKEYS
PEER_CHUNK = 256
PEER_NTOP = PEER_TOPK + 1
_CANDS = tuple((a, b) for a in range(PEER_NTOP) for b in range(PEER_NTOP)
               if (a + 1) * (b + 1) <= PEER_NTOP)


def _sorting_network(n):
    pairs = []
    p = 1
    while p < n:
        k = p
        while k >= 1:
            for j in range(k % p, n - k, 2 * k):
                for i in range(min(k, n - j - k)):
                    if (i + j) // (2 * p) == (i + j + k) // (2 * p):
                        pairs.append((i + j, i + j + k))
            k //= 2
        p *= 2
    return tuple(pairs)


def _top_values(x, n):
    n_tiles = x.shape[0] // SUBLANES
    rows = [x[r * SUBLANES:(r + 1) * SUBLANES] for r in range(n_tiles)]
    for a, b in _sorting_network(n_tiles):
        rows[a], rows[b] = jnp.maximum(rows[a], rows[b]), jnp.minimum(rows[a], rows[b])
    out = []
    for r in range(n):
        m = jnp.max(rows[0], axis=0, keepdims=True)
        out.append(m)
        taken = rows[0] == m
        for q in range(min(n_tiles, n - 1 - r)):
            below = rows[q + 1] if q + 1 < n_tiles else NEG_INF
            rows[q] = jnp.where(taken, below, rows[q])
    return out


def _peer_kernel(h_ref, g_ref, wqt_ref, keys_ref, u_ref, vt_ref, vprev_ref, o_ref,
                 xn_ref, s_ref, e1_ref, se2_ref, top_ref, tau_ref, acc_ref, ht_ref, act_ref, *, tt):
    eb = pl.program_id(1)
    nlb = tt // LANES
    n_ck = PEER_EB // PEER_CHUNK
    chunk = lambda ck: slice(ck * PEER_CHUNK, (ck + 1) * PEER_CHUNK)

    @pl.when(eb == 0)
    def _prepare():
        xn = _rms(h_ref[...], g_ref[...])
        xh, xl = _split(xn)
        xn_ref[...] = (xn * (2.0 ** -0.5)).astype(BF16)
        qt = (_dot_nt(wqt_ref[0], xh) + _dot_nt(wqt_ref[0], xl) + _dot_nt(wqt_ref[1], xh))
        for hc in range(2 * PEER_HEADS):
            qh, ql = _split(qt[hc * PEER_NKEYS:(hc + 1) * PEER_NKEYS, :])
            s_ref[hc] = (jnp.dot(keys_ref[0, hc], qh, preferred_element_type=F32)
                         + jnp.dot(keys_ref[0, hc], ql, preferred_element_type=F32)
                         + jnp.dot(keys_ref[1, hc], qh, preferred_element_type=F32))

        def per_lane_block(lb, carry):
            lanes = pl.ds(pl.multiple_of(lb * LANES, LANES), LANES)
            for hc in range(2 * PEER_HEADS):
                h, c = divmod(hc, 2)
                work = s_ref[hc, :, lanes]
                ex = jnp.exp(work - jnp.max(work, axis=0, keepdims=True))
                if c == 0:
                    e1_ref[h, :, lanes] = ex
                else:
                    for rb in range(PEER_NKEYS // SUBLANES):
                        rows = slice(rb * SUBLANES, (rb + 1) * SUBLANES)
                        se2_ref[lb, rb, 2 * h] = work[rows]
                        se2_ref[lb, rb, 2 * h + 1] = ex[rows]
                for r, m in enumerate(_top_values(work, PEER_NTOP)):
                    top_ref[c, r, pl.ds(h, 1), lanes] = m
            v1 = [top_ref[0, r, :, lanes] for r in range(PEER_NTOP)]
            v2 = [top_ref[1, r, :, lanes] for r in range(PEER_NTOP)]
            cands = [v1[a] + v2[b] for (a, b) in _CANDS]
            cv = []
            for r in range(PEER_TOPK + 1):
                m = functools.reduce(jnp.maximum, cands)
                cv.append(m)
                if r < PEER_TOPK:
                    cands = [jnp.where(x == m, NEG_INF, x) for x in cands]
            z = functools.reduce(lambda a, b: a + b, [jnp.exp(x - cv[0]) for x in cv[:PEER_TOPK]])
            tau_ref[0, :, lanes] = 0.5 * (cv[PEER_TOPK - 1] + cv[PEER_TOPK])
            tau_ref[1, :, lanes] = (2.0 ** -0.5) / z
            return carry

        lax.fori_loop(0, nlb, per_lane_block, 0)
        acc_ref[...] = jnp.zeros_like(acc_ref)
        act_ref[PEER_EB // 2:, :] = jnp.zeros((PEER_EB // 2, tt), BF16)

    irows =pl.ds(pl.multiple_of(eb * PEER_ROWS, PEER_ROWS), PEER_ROWS)
    shape = (SUBLANES, LANES)

    def gate_rows(il, lb):
        lanes = slice(lb * LANES, (lb + 1) * LANES)
        tau = tau_ref[0, :, lanes]
        rz = tau_ref[1, :, lanes]
        theta, e1 = [], []
        for h in range(PEER_HEADS):
            theta.append(jnp.broadcast_to(tau[h:h + 1] - s_ref[2 * h, irows, lanes][il:il + 1], shape))
            e1.append(jnp.broadcast_to(e1_ref[h, irows, lanes][il:il + 1] * rz[h:h + 1], shape))
        for jb in range(PEER_NKEYS // BF16_ROWS):
            pieces = []
            for half in range(2):
                rb = 2 * jb + half
                j0 = rb * SUBLANES
                terms = [jnp.where(se2_ref[lb, rb, 2 * h] >= theta[h],
                                   e1[h] * se2_ref[lb, rb, 2 * h + 1], 0.0) for h in range(PEER_HEADS)]
                while len(terms) > 1:
                    terms = [terms[i] + terms[i + 1] for i in range(0, len(terms), 2)]
                r0 = il * PEER_NKEYS + j0
                hv = ht_ref[r0:r0 + SUBLANES, lanes]
                pieces.append(hv * (1.0 + lax.erf(hv)) * terms[0])
            a0 = il * PEER_NKEYS + jb * BF16_ROWS
            act_ref[a0:a0 + BF16_ROWS, lanes] = jnp.concatenate(pieces, axis=0).astype(BF16)

    lo, hi = slice(0, PEER_EB // 2), slice(PEER_EB // 2, PEER_EB)

    def hval(ck):
        ht_ref[chunk(ck), :] = _dot_nt(u_ref[chunk(ck), :], xn_ref[...])

    def gates(cks):
        for ck in cks:
            for il in range(ck * PEER_CHUNK // PEER_NKEYS, (ck + 1) * PEER_CHUNK // PEER_NKEYS):
                for lb in range(nlb):
                    gate_rows(il, lb)

    first, second = range(n_ck // 2), range(n_ck // 2, n_ck)
    for ck in first:
        hval(ck)
    acc_ref[...] += jnp.dot(vprev_ref[...], act_ref[hi, :], preferred_element_type=F32)
    for ck in second:
        hval(ck)
    gates(first)
    acc_ref[...] += jnp.dot(vt_ref[:, lo], act_ref[lo, :], preferred_element_type=F32)
    gates(second)

    @pl.when(eb == pl.num_programs(1) - 1)
    def _finish():
        tail = jnp.dot(vt_ref[:, hi], act_ref[hi, :], preferred_element_type=F32)
        o_ref[...] = h_ref[...] + (acc_ref[...] + tail).T


def _peer(h, norm_g, wqt, keys, u, vt):
    t_total = h.shape[0]
    tt = min(PEER_TT, t_total)
    assert t_total % tt == 0 and tt % LANES == 0
    n_eb = PEER_EXPERTS // PEER_EB
    n_ck = PEER_EB // PEER_CHUNK
    kern = functools.partial(_peer_kernel, tt=tt)
    return pl.pallas_call(
        kern,
        grid=(t_total // tt, n_eb),
        in_specs=[
            pl.BlockSpec((tt, D_MODEL), lambda t, e: (t, 0)),
            pl.BlockSpec((1, D_MODEL), lambda t, e: (0, 0)),
            pl.BlockSpec((2, PEER_HEADS * PEER_DKEY, D_MODEL), lambda t, e: (0, 0, 0)),
            pl.BlockSpec((2, 2 * PEER_HEADS, PEER_NKEYS, PEER_DKEY // 2), lambda t, e: (0, 0, 0, 0)),
            pl.BlockSpec((PEER_EB, D_MODEL), lambda t, e: (e, 0)),
            pl.BlockSpec((D_MODEL, PEER_EB), lambda t, e: (0, e)),
            pl.BlockSpec((D_MODEL, PEER_EB // 2), lambda t, e: (0, jnp.maximum(2 * e - 1, 0))),
        ],
        out_specs=pl.BlockSpec((tt, D_MODEL), lambda t, e: (t, 0)),
        out_shape=jax.ShapeDtypeStruct((t_total, D_MODEL), F32),
        scratch_shapes=[
            pltpu.VMEM((tt, D_MODEL), BF16),
            pltpu.VMEM((2 * PEER_HEADS, PEER_NKEYS, tt), F32),
            pltpu.VMEM((PEER_HEADS, PEER_NKEYS, tt), F32),
            pltpu.VMEM((tt // LANES, PEER_NKEYS // SUBLANES, 2 * PEER_HEADS, SUBLANES, LANES), F32),
            pltpu.VMEM((2, PEER_NTOP, PEER_HEADS, tt), F32),
            pltpu.VMEM((2, PEER_HEADS, tt), F32),
            pltpu.VMEM((D_MODEL, tt), F32),
            pltpu.VMEM((PEER_EB, tt), F32),
            pltpu.VMEM((PEER_EB, tt), BF16),
        ],
        compiler_params=_cparams("parallel", "arbitrary"),
        name="peer",
    )(h, norm_g, wqt, keys, u, vt, vt)


TOKEN_TILE = 512


def _inproj_kernel(x_ref, g_ref, wr_ref, ww_ref, pr_ref, pw_ref):
    xn = _rms(x_ref[...], g_ref[...]).astype(BF16)
    pr_ref[...] = jnp.dot(xn, wr_ref[...], preferred_element_type=F32)
    pw_ref[...] = jnp.dot(xn, ww_ref[...], preferred_element_type=F32)


def _inproj(x, norm_g, w_ret, w_rw):
    t_total = x.shape[0]
    tm = min(TOKEN_TILE, t_total)
    assert t_total % tm == 0
    const = lambda shape: pl.BlockSpec(shape, lambda t: (0,) * len(shape))
    return pl.pallas_call(
        _inproj_kernel,
        grid=(t_total // tm,),
        in_specs=[pl.BlockSpec((tm, D_MODEL), lambda t: (t, 0)), const((1, D_MODEL)),
                  const((D_MODEL, RET_PROJ)), const((D_MODEL, RWKV_PROJ))],
        out_specs=[pl.BlockSpec((tm, RET_PROJ), lambda t: (t, 0)),
                   pl.BlockSpec((tm, RWKV_PROJ), lambda t: (t, 0))],
        out_shape=[jax.ShapeDtypeStruct((t_total, RET_PROJ), F32),
                   jax.ShapeDtypeStruct((t_total, RWKV_PROJ), F32)],
        compiler_params=_cparams("parallel"),
        name="inproj",
    )(x, norm_g, w_ret, w_rw)


def _rope_partner(x):
    lane = lax.broadcasted_iota(jnp.int32, x.shape, 1)
    up = pltpu.roll(x, x.shape[1] - RET_DK // 2, 1)
    down = pltpu.roll(x, RET_DK // 2, 1)
    return jnp.where(lane % RET_DK < RET_DK // 2, up, down)


def _retention_kernel(p_ref, s0_ref, cos_ref, sin_ref, dmat_ref, xz_ref, ones_ref, o_ref, so_ref, s_ref,
                      *, nb, g_chunk):
    ci = pl.program_id(1)
    qk = RET_HEADS * RET_DK
    n_pairs = RET_HEADS // 2
    row_head = lax.broadcasted_iota(jnp.int32, (LANES, LANES), 0) // RET_DK
    lane_head = lax.broadcasted_iota(jnp.int32, (LANES, LANES), 1) // RET_DV
    same_head = row_head == lane_head

    @pl.when(ci == 0)
    def _init():
        s_ref[...] = jnp.zeros_like(s_ref)
        for p in range(n_pairs):
            s_ref[:, p, :RET_DK, :RET_DV] = s0_ref[:, 2 * p]
            s_ref[:, p, RET_DK:, RET_DV:] = s0_ref[:, 2 * p + 1]

    cos = cos_ref[...]
    sin = sin_ref[...]

    def per_seq(n, carry):
        x = p_ref[n]
        q = x[:, :qk]
        k = x[:, qk:2 * qk]
        q = q * cos + _rope_partner(q) * sin
        k = (k * cos + _rope_partner(k) * sin) * (RET_DK ** -0.5)
        lane_is_hi = lax.broadcasted_iota(jnp.int32, (q.shape[0], LANES), 1) >= RET_DK
        for p in range(n_pairs):
            ls = slice(p * LANES, (p + 1) * LANES)
            qp, kp, vp = q[:, ls], k[:, ls], x[:, 2 * qk + p * LANES:2 * qk + (p + 1) * LANES]
            kb, vb = kp.astype(BF16), vp.astype(BF16)
            s_p = s_ref[n, p]
            o = jnp.dot(qp.astype(BF16), s_p.astype(BF16), preferred_element_type=F32) * xz_ref[0, p]
            for hh in range(2):
                mine = lane_is_hi if hh else jnp.logical_not(lane_is_hi)
                inner = _dot_nt(jnp.where(mine, qp, 0.0).astype(BF16), kb) * dmat_ref[2 * p + hh]
                o = o + jnp.dot(inner.astype(BF16), jnp.where(mine, vp, 0.0).astype(BF16),
                                preferred_element_type=F32)
            o_ref[n, :, ls] = o
            decay = jnp.where(row_head == 0, g_chunk[2 * p], g_chunk[2 * p + 1])
            grown = _dot_tn((kp * xz_ref[1, p]).astype(BF16), vb)
            s_ref[n, p] = s_p * decay + jnp.where(same_head, grown, 0.0)
        o = o_ref[n]
        g = x[:, 2 * qk + RET_W:]
        oc = o - _head_sums(o, ones_ref[...]) * (1.0 / RET_DV)
        var = _head_sums(oc * oc, ones_ref[...]) * (1.0 / RET_DV)
        o_ref[n] = oc * lax.rsqrt(var + GN_EPS) * (g * jax.nn.sigmoid(g))
        return carry

    lax.fori_loop(0, nb, per_seq, 0, unroll=2)

    @pl.when(ci == pl.num_programs(1) - 1)
    def _done():
        for p in range(n_pairs):
            so_ref[:, 2 * p] = s_ref[:, p, :RET_DK, :RET_DV]
            so_ref[:, 2 * p + 1] = s_ref[:, p, RET_DK:, RET_DV:]


def _retention(proj_ret, s0, pos, head_ones):
    assert RET_W == RWKV_W and RET_DV == RWKV_N
    b, l, _ = proj_ret.shape
    c = RET_CHUNK if l % RET_CHUNK == 0 else l
    nb = 2 if c == RET_CHUNK else 16
    assert b % nb == 0
    half = RET_DK // 2
    inv = ROPE_BASE ** (-jnp.arange(half, dtype=F32) / half)
    ang = pos.astype(F32)[:, None] * inv[None, :]
    cos = jnp.tile(jnp.cos(ang), (1, 2 * RET_HEADS))
    sin = jnp.tile(jnp.concatenate([-jnp.sin(ang), jnp.sin(ang)], axis=1), (1, RET_HEADS))
    lg = jnp.log(1.0 - 2.0 ** (-5.0 - jnp.arange(RET_HEADS, dtype=F32)))
    idx = jnp.arange(c, dtype=F32)
    diff = idx[:, None] - idx[None, :]
    dmat = jnp.where(diff[None] >= 0, jnp.exp(jnp.maximum(diff, 0.0)[None] * lg[:, None, None]), 0.0)
    xi = jnp.exp((idx + 1.0)[None, :] * lg[:, None])
    zeta = jnp.exp((c - 1.0 - idx)[None, :] * lg[:, None])
    pair_lanes = lambda a: jnp.repeat(a.reshape(RET_HEADS // 2, 2, c).transpose(0, 2, 1), RET_DV, axis=-1)
    xz = jnp.stack([pair_lanes(xi), pair_lanes(zeta)])
    g_chunk = tuple(math.exp(c * math.log(1.0 - 2.0 ** (-5.0 - h))) for h in range(RET_HEADS))
    kern = functools.partial(_retention_kernel, nb=nb, g_chunk=g_chunk)
    return pl.pallas_call(
        kern,
        grid=(b // nb, l // c),
        in_specs=[
            pl.BlockSpec((nb, c, RET_PROJ), lambda i, j: (i, j, 0)),
            pl.BlockSpec((nb, RET_HEADS, RET_DK, RET_DV), lambda i, j: (i, 0, 0, 0)),
            pl.BlockSpec((c, RET_HEADS * RET_DK), lambda i, j: (j, 0)),
            pl.BlockSpec((c, RET_HEADS * RET_DK), lambda i, j: (j, 0)),
            pl.BlockSpec((RET_HEADS, c, c), lambda i, j: (0, 0, 0)),
            pl.BlockSpec((2, RET_HEADS // 2, c, LANES), lambda i, j: (0, 0, 0, 0)),
            pl.BlockSpec((RET_W, RET_W), lambda i, j: (0, 0)),
        ],
        out_specs=[pl.BlockSpec((nb, c, RET_W), lambda i, j: (i, j, 0)),
                   pl.BlockSpec((nb, RET_HEADS, RET_DK, RET_DV), lambda i, j: (i, 0, 0, 0))],
        out_shape=[jax.ShapeDtypeStruct((b, l, RET_W), F32),
                   jax.ShapeDtypeStruct((b, RET_HEADS, RET_DK, RET_DV), F32)],
        scratch_shapes=[pltpu.VMEM((nb, RET_HEADS // 2, 2 * RET_DK, 2 * RET_DV), F32)],
        compiler_params=_cparams("parallel", "arbitrary"),
        name="retention",
    )(proj_ret, s0, cos, sin, dmat, xz, head_ones)


def _softplus(x):
    return jnp.maximum(x, 0.0) + jnp.log1p(jnp.exp(-jnp.abs(x)))


def _rwkv_prep_kernel(x_ref, sh_ref, mu_ref, w0_ref, w2_ref, a0_ref, a2_ref, g2_ref, kks_ref, ka_ref,
                      rk_ref, ones_ref, kkw_ref, cck_ref, rv_ref, gate_ref, bonus_ref,
                      carry_ref, *, nb, tb):
    ti = pl.program_id(1)
    x = x_ref[...]
    first = jnp.where(ti == 0, sh_ref[...], carry_ref[...])
    prev = pltpu.roll(x.reshape(nb * tb, RWKV_PROJ), 1, 0).reshape(nb, tb, RWKV_PROJ)
    tpos = lax.broadcasted_iota(jnp.int32, x.shape, 1)
    prev = jnp.where(tpos == 0, first, prev)
    carry_ref[...] = x[:, tb - 1:tb, :]
    f = (x + (prev - x) * mu_ref[...]).reshape(nb * tb, RWKV_PROJ)
    o1, o2, o3 = RWKV_W, 2 * RWKV_W, 3 * RWKV_W
    r, k7, v7 = f[:, :o1], f[:, o1:o2], f[:, o2:o3]
    fw = f[:, o3:o3 + DECAY_LORA]
    fa = f[:, o3 + DECAY_LORA:o3 + DECAY_LORA + AAA_LORA]
    fg = f[:, o3 + DECAY_LORA + AAA_LORA:]
    w = -_softplus(-(w0_ref[...] + _dot3(jnp.tanh(fw), w2_ref[...]))) - 0.5
    a = jax.nn.sigmoid(a0_ref[...] + _dot3(fa, a2_ref[...]))
    gate = _dot3(jax.nn.sigmoid(fg), g2_ref[...])
    kk = k7 * kks_ref[...]
    kk = kk / jnp.maximum(jnp.sqrt(_head_sums(kk * kk, ones_ref[...])), 1e-12)
    k = k7 * (1.0 + (a - 1.0) * ka_ref[...])
    out3 = lambda y: y.reshape(nb, tb, RWKV_W)
    kkw_ref[:, :, :RWKV_W] = out3(kk)
    kkw_ref[:, :, RWKV_W:] = out3(jnp.exp(-jnp.exp(w)))
    cck_ref[:, :, :RWKV_W] = out3(kk * a)
    cck_ref[:, :, RWKV_W:] = out3(k)
    rv_ref[:, :, :RWKV_W] = out3(r)
    rv_ref[:, :, RWKV_W:] = out3(v7)
    gate_ref[...] = out3(gate)
    bonus_ref[...] = out3(_head_sums(r * k * rk_ref[...], ones_ref[...]) * v7)


def _rwkv_prep(proj_rw, shift, wt):
    b, l, _ = proj_rw.shape
    tb = min(TOKEN_TILE, l)
    nb = max(1, 128 // tb)
    assert l % tb == 0 and b % nb == 0 and tb % SUBLANES == 0
    const = lambda shape: pl.BlockSpec(shape, lambda i, j: (0,) * len(shape))
    blk = pl.BlockSpec((nb, tb, RWKV_W), lambda i, j: (i, j, 0))
    blk2 = pl.BlockSpec((nb, tb, 2 * RWKV_W), lambda i, j: (i, j, 0))
    kern = functools.partial(_rwkv_prep_kernel, nb=nb, tb=tb)
    return pl.pallas_call(
        kern,
        grid=(b // nb, l // tb),
        in_specs=[
            pl.BlockSpec((nb, tb, RWKV_PROJ), lambda i, j: (i, j, 0)),
            pl.BlockSpec((nb, 1, RWKV_PROJ), lambda i, j: (i, 0, 0)),
            const((1, RWKV_PROJ)), const((1, RWKV_W)), const((DECAY_LORA, RWKV_W)),
            const((1, RWKV_W)), const((AAA_LORA, RWKV_W)), const((GATE_LORA, RWKV_W)),
            const((1, RWKV_W)), const((1, RWKV_W)), const((1, RWKV_W)), const((RWKV_W, RWKV_W)),
        ],
        out_specs=[blk2] * 3 + [blk] * 2,
        out_shape=[jax.ShapeDtypeStruct((b, l, 2 * RWKV_W), F32)] * 3 + [jax.ShapeDtypeStruct((b, l, RWKV_W), F32)] * 2,
        scratch_shapes=[pltpu.VMEM((nb, 1, RWKV_PROJ), F32)],
        compiler_params=_cparams("parallel", "arbitrary"),
        name="rwkv_prep",
    )(proj_rw, shift.reshape(b, 1, RWKV_PROJ), wt['mu'], wt['w0'], wt['w2'], wt['a0'], wt['a2'],
      wt['g2'], wt['kk'], wt['ka'], wt['rk'], wt['head_ones'])


def _wkv_kernel(*refs, tb, ni, paired):
    n_in = 3 if paired else 6
    in_refs, (s0_ref, y_ref, so_ref, s_ref, j_ref, v_ref) = refs[:n_in], refs[n_in:]
    ti = pl.program_id(1)

    @pl.when(ti == 0)
    def _init():
        s_ref[...] = s0_ref[0]

    if paired:
        def spread(t, carry):
            low = lax.broadcasted_iota(jnp.int32, (RWKV_N, LANES), 1) < LANES // 2
            low_v = lax.broadcasted_iota(jnp.int32, (ni, LANES), 1) < LANES // 2
            kkw, cck, rv = (ref[0, t] for ref in in_refs)
            kkw_sw, cck_sw, rv_sw = (pltpu.roll(x, LANES // 2, 1) for x in (kkw, cck, rv))
            j_ref[t, 0], j_ref[t, 1] = jnp.where(low, kkw, kkw_sw), jnp.where(low, kkw_sw, kkw)
            j_ref[t, 2], j_ref[t, 3] = jnp.where(low, cck, cck_sw), jnp.where(low, cck_sw, cck)
            j_ref[t, 4] = jnp.where(low, rv, rv_sw)
            v_ref[t] = jnp.where(low_v, rv_sw[:ni], rv[ni:])
            return carry

        lax.fori_loop(0, tb, spread, 0, unroll=8)

    def step(t, carry):
        if paired:
            kk, w, cc, k, r = (j_ref[t, i] for i in range(5))
            vrow = lambda ii: v_ref[t, ii:ii + 1, :]
        else:
            kk, w, cc, k, r = (ref[0, t] for ref in in_refs[:5])
            vrow = lambda ii: in_refs[5][0, t, ii:ii + 1, :]
        for ii in range(ni):
            s = s_ref[ii]
            sa = -jnp.sum(s * kk, axis=0, keepdims=True)
            sn = s * w + sa * cc + vrow(ii) * k
            s_ref[ii] = sn
            y_ref[0, t, ii:ii + 1, :] = jnp.sum(sn * r, axis=0, keepdims=True)
        return carry

    lax.fori_loop(0, tb, step, 0)

    @pl.when(ti == pl.num_programs(1) - 1)
    def _done():
        so_ref[0] = s_ref[...]


def _wkv_scan(inputs, s0):
    paired = len(inputs) == 3
    g, ni = s0.shape[:2]
    l = inputs[0].shape[1]
    tb = min(64, l)
    assert l % tb == 0 and (not paired or 2 * ni == RWKV_N)
    jblk = pl.BlockSpec((1, tb, RWKV_N, LANES), lambda i, j: (i, j, 0, 0))
    iblk = pl.BlockSpec((1, tb, ni, LANES), lambda i, j: (i, j, 0, 0))
    sblk = pl.BlockSpec((1, ni, RWKV_N, LANES), lambda i, j: (i, 0, 0, 0))
    kern = functools.partial(_wkv_kernel, tb=tb, ni=ni, paired=paired)
    return pl.pallas_call(
        kern,
        grid=(g, l // tb),
        in_specs=([jblk] * 3 if paired else [jblk] * 5 + [iblk]) + [sblk],
        out_specs=[iblk, sblk],
        out_shape=[jax.ShapeDtypeStruct((g, l, ni, LANES), F32),
                   jax.ShapeDtypeStruct((g, ni, RWKV_N, LANES), F32)],
        scratch_shapes=[pltpu.VMEM((ni, RWKV_N, LANES), F32),
                        pltpu.VMEM((tb if paired else 1, 5, RWKV_N, LANES), F32),
                        pltpu.VMEM((tb if paired else 1, ni, LANES), F32)],
        compiler_params=_cparams("parallel", "arbitrary"),
        name="wkv_scan",
    )(*inputs, s0)


def _outproj_kernel(x_ref, oret_ref, y_ref, bonus_ref, gate_ref, lng_ref, lnb_ref, ones_ref,
                    wtop_ref, wbot_ref, o_ref):
    y = y_ref[...]
    yc = y - _head_sums(y, ones_ref[...]) * (1.0 / RWKV_N)
    var = _head_sums(yc * yc, ones_ref[...]) * (1.0 / RWKV_N)
    o = yc * lax.rsqrt(var + RWKV_LN_EPS) * lng_ref[...] + lnb_ref[...]
    o = (o + bonus_ref[...]) * gate_ref[...]
    o_ref[...] = (x_ref[...]
                  + jnp.dot(oret_ref[...].astype(BF16), wtop_ref[...], preferred_element_type=F32)
                  + jnp.dot(o.astype(BF16), wbot_ref[...], preferred_element_type=F32))


def _outproj(x, o_ret, y, bonus, gate, wt):
    t_total = x.shape[0]
    tm = min(TOKEN_TILE, t_total)
    const = lambda shape: pl.BlockSpec(shape, lambda t: (0,) * len(shape))
    half = pl.BlockSpec((tm, RWKV_W), lambda t: (t, 0))
    full = pl.BlockSpec((tm, D_MODEL), lambda t: (t, 0))
    return pl.pallas_call(
        _outproj_kernel,
        grid=(t_total // tm,),
        in_specs=[full, half, half, half, half, const((1, RWKV_W)), const((1, RWKV_W)),
                  const((RWKV_W, RWKV_W)), const((RET_W, D_MODEL)), const((RWKV_W, D_MODEL))],
        out_specs=full,
        out_shape=jax.ShapeDtypeStruct((t_total, D_MODEL), F32),
        compiler_params=_cparams("parallel"),
        name="outproj",
    )(x, o_ret, y, bonus, gate, wt['ln_g'], wt['ln_b'], wt['head_ones'], wt['w_out_top'], wt['w_out_bot'])


def _ple_kernel(h_ref, p_ref, gn_ref, wg_ref, wp_ref, fn_ref, o_ref):
    h = h_ref[...]
    gate = jax.nn.sigmoid(jnp.dot(_rms(h, gn_ref[...]).astype(BF16), wg_ref[...], preferred_element_type=F32))
    h = h + gate * jnp.dot(p_ref[...].astype(BF16), wp_ref[...], preferred_element_type=F32)
    o_ref[...] = _rms(h, fn_ref[...])


def _ple(h, p, wt):
    t_total = h.shape[0]
    tm = min(TOKEN_TILE, t_total)
    const = lambda shape: pl.BlockSpec(shape, lambda t: (0,) * len(shape))
    full = pl.BlockSpec((tm, D_MODEL), lambda t: (t, 0))
    return pl.pallas_call(
        _ple_kernel,
        grid=(t_total // tm,),
        in_specs=[full, pl.BlockSpec((tm, PLE_DIM), lambda t: (t, 0)), const((1, D_MODEL)),
                  const((D_MODEL, D_MODEL)), const((PLE_DIM, D_MODEL)), const((1, D_MODEL))],
        out_specs=full,
        out_shape=jax.ShapeDtypeStruct((t_total, D_MODEL), F32),
        compiler_params=_cparams("parallel"),
        name="ple",
    )(h, p, wt['ple_norm'], wt['ple_gate'], wt['ple_proj'], wt['norm_final'])


def _prep_weights(p):
    row = lambda a: a.reshape(1, -1)
    lane_head = jnp.arange(RWKV_W) // RWKV_N
    return {
        'norm_mix': row(p['norm_mix'][0]),
        'w_ret': p['w_in'][0][:, :RET_PROJ].astype(BF16),
        'w_rw': p['w_in'][0][:, RET_PROJ:].astype(BF16),
        'mu': row(p['rw_mu'][0]), 'w0': row(p['rw_w0'][0]), 'w2': p['rw_w2'][0],
        'a0': row(p['rw_a0'][0]), 'a2': p['rw_a2'][0], 'g2': p['rw_g2'][0],
        'kk': row(p['rw_kk'][0]), 'ka': row(p['rw_ka'][0]), 'rk': row(p['rw_rk'][0]),
        'ln_g': row(p['rw_ln_g'][0]), 'ln_b': row(p['rw_ln_b'][0]),
        'head_ones': (lane_head[:, None] == lane_head[None, :]).astype(BF16),
        'w_out_top': p['w_out'][0][:RET_W].astype(BF16),
        'w_out_bot': p['w_out'][0][RET_W:].astype(BF16),
        'norm_ffn': row(p['norm_ffn'][0]),
        'wqt': jnp.stack(_split(p['peer_wq'][0].T)),
        'keys': jnp.stack(_split(p['peer_keys'][0].reshape(2 * PEER_HEADS, PEER_NKEYS, PEER_DKEY // 2))),
        'u': p['peer_u'][0].astype(BF16),
        'vt': p['peer_v'][0].astype(BF16).T,
        'ple_norm': row(p['ple_norm'][0]),
        'ple_gate': p['ple_gate'][0].astype(BF16),
        'ple_proj': p['ple_proj'][0].astype(BF16),
        'norm_final': row(p['norm_final']),
    }


def _scan_inputs(pairs, b):
    l = pairs[0].shape[1]
    if b * RWKV_HEADS * 2 == LANES:
        return [p.reshape(b, l, 2, RWKV_HEADS, RWKV_N).transpose(1, 4, 2, 0, 3).reshape(1, l, RWKV_N, LANES)
                for p in pairs]
    assert b == LANES
    split = lambda p: [h.reshape(b, l, RWKV_HEADS, RWKV_N).transpose(2, 1, 3, 0)
                       for h in (p[..., :RWKV_W], p[..., RWKV_W:])]
    (kk, w), (cc, k), (r, v) = (split(p) for p in pairs)
    return [kk, w, cc, k, r, v]


def _values_from_lanes(y, b):
    g, l, ni, _ = y.shape
    if g == 1:
        y = y.reshape(l, ni, 2, b, RWKV_HEADS).transpose(3, 0, 4, 2, 1)
    else:
        y = y.transpose(3, 1, 0, 2)
    return y.reshape(b, l, RWKV_W)


def _state_to_lanes(s):
    b = s.shape[0]
    if b * RWKV_HEADS * 2 == LANES:
        s = s.reshape(b, RWKV_HEADS, 2, RWKV_N // 2, RWKV_N).transpose(3, 4, 2, 0, 1)
        return s.reshape(1, RWKV_N // 2, RWKV_N, LANES)
    return s.transpose(1, 2, 3, 0)


def _state_from_lanes(s, b):
    g, ni, _, _ = s.shape
    if g == 1:
        s = s.reshape(ni, RWKV_N, 2, b, RWKV_HEADS).transpose(3, 4, 2, 0, 1)
        return s.reshape(b, RWKV_HEADS, RWKV_N, RWKV_N)
    return s.transpose(3, 0, 1, 2)


def _layer(x, p_l, pos, s_ret, s_wkv, s_shift, wt):
    b, l, _ = x.shape
    x2 = x.reshape(b * l, D_MODEL)
    proj_ret, proj_rw = _inproj(x2, wt['norm_mix'], wt['w_ret'], wt['w_rw'])
    o_ret, s_ret_new = _retention(proj_ret.reshape(b, l, RET_PROJ), s_ret, pos, wt['head_ones'])
    proj_rw = proj_rw.reshape(b, l, RWKV_PROJ)
    new_shift = proj_rw[:, -1]
    kkw, cck, rv, gate, bonus = _rwkv_prep(proj_rw, s_shift, wt)
    y, s_lanes = _wkv_scan(_scan_inputs((kkw, cck, rv), b), _state_to_lanes(s_wkv))
    y = _values_from_lanes(y, b)
    s_wkv_new = _state_from_lanes(s_lanes, b)
    flat = lambda a: a.reshape(b * l, -1)
    h = _outproj(x2, flat(o_ret), flat(y), flat(bonus), flat(gate), wt)
    h = _peer(h, wt['norm_ffn'], wt['wqt'], wt['keys'], wt['u'], wt['vt'])
    y_out = _ple(h, p_l.reshape(b * l, PLE_DIM), wt)
    return y_out.reshape(b, l, D_MODEL), s_ret_new, s_wkv_new, new_shift


def kernel(x_prompt, x_sample, state_ret, state_wkv, state_shift, p_prompt, p_sample, norm_mix, w_in, rw_mu, rw_w0, rw_w2, rw_a0, rw_a2, rw_g2, rw_kk, rw_ka, rw_rk, rw_ln_g, rw_ln_b, w_out, norm_ffn, peer_wq, peer_keys, peer_u, peer_v, ple_norm, ple_gate, ple_proj, norm_final):
    assert norm_mix.shape[0] == 1, "single-layer trunk"
    wt = _prep_weights(dict(
        norm_mix=norm_mix, w_in=w_in, rw_mu=rw_mu, rw_w0=rw_w0, rw_w2=rw_w2, rw_a0=rw_a0, rw_a2=rw_a2,
        rw_g2=rw_g2, rw_kk=rw_kk, rw_ka=rw_ka, rw_rk=rw_rk, rw_ln_g=rw_ln_g, rw_ln_b=rw_ln_b, w_out=w_out,
        norm_ffn=norm_ffn, peer_wq=peer_wq, peer_keys=peer_keys, peer_u=peer_u, peer_v=peer_v,
        ple_norm=ple_norm, ple_gate=ple_gate, ple_proj=ple_proj, norm_final=norm_final))
    bp, lp, _ = x_prompt.shape
    bs, ls, _ = x_sample.shape
    pos_p = jnp.arange(lp, dtype=jnp.int32)
    pos_s = PAST_LEN + jnp.arange(ls, dtype=jnp.int32)
    zero_ret = jnp.zeros((bp, RET_HEADS, RET_DK, RET_DV), F32)
    zero_wkv = jnp.zeros((bp, RWKV_HEADS, RWKV_N, RWKV_N), F32)
    zero_shift = jnp.zeros((bp, RWKV_PROJ), F32)
    yp, rp, wp, sp = _layer(x_prompt, p_prompt[0], pos_p, zero_ret, zero_wkv, zero_shift, wt)
    ys, rs, ws, ss = _layer(x_sample, p_sample[0], pos_s, state_ret[0], state_wkv[0], state_shift[0], wt)
    return (yp, ys, rp[None], wp[None], sp[None], rs[None], ws[None], ss[None])
```

```python
import functools
import math

import jax
import jax.numpy as jnp
from jax import lax
from jax.experimental import pallas as pl
from jax.experimental.pallas import tpu as pltpu

F32 = jnp.float32
BF16 = jnp.bfloat16

D_MODEL = 1024
PAST_LEN = 16384
RET_HEADS = 8
RET_DK = 64
RET_DV = 64
RET_W = RET_HEADS * RET_DV
RWKV_HEADS = 8
RWKV_N = 64
RWKV_W = RWKV_HEADS * RWKV_N
DECAY_LORA = 64
AAA_LORA = 64
GATE_LORA = 128
RET_PROJ = 2 * RET_HEADS * RET_DK + 2 * RET_W
RWKV_PROJ = 3 * RWKV_W + DECAY_LORA + AAA_LORA + GATE_LORA
RET_CHUNK = 128
ROPE_BASE = 10000.0
PEER_HEADS = 8
PEER_NKEYS = 128
PEER_EXPERTS = PEER_NKEYS * PEER_NKEYS
PEER_DKEY = 256
PEER_TOPK = 16
PLE_DIM = 256
RMS_EPS = 1e-6
GN_EPS = 1e-5
RWKV_LN_EPS = 64e-5

LANES = 128
SUBLANES = 8
BF16_ROWS = 16
VMEM_LIMIT = 56 * 1024 * 1024

NEG_INF = float("-inf")


def _cparams(*sem):
    return pltpu.CompilerParams(dimension_semantics=sem, vmem_limit_bytes=VMEM_LIMIT)


def _rms(x, g):
    return x * lax.rsqrt(jnp.mean(x * x, axis=-1, keepdims=True) + RMS_EPS) * g


def _split(x):
    hi = x.astype(BF16)
    return hi, (x - hi.astype(F32)).astype(BF16)


def _dot3(a, b):
    ah, al = _split(a)
    bh, bl = _split(b)
    dot = functools.partial(jnp.dot, preferred_element_type=F32)
    return dot(ah, bh) + dot(ah, bl) + dot(al, bh)


def _head_sums(x, head_ones):
    xh, xl = _split(x)
    dot = functools.partial(jnp.dot, preferred_element_type=F32)
    return dot(xh, head_ones) + dot(xl, head_ones)


def _dot_nt(a, b):
    return lax.dot_general(a, b, (((1,), (1,)), ((), ())), preferred_element_type=F32)


def _dot_tn(a, b):
    return lax.dot_general(a, b, (((0,), (0,)), ((), ())), preferred_element_type=F32)


PEER_TT = 512
PEER_ROWS = 8
PEER_EB = PEER_ROWS * PEER_NKEYS
PEER_CHUNK = 256
PEER_NTOP = PEER_TOPK + 1
_CANDS = tuple((a, b) for a in range(PEER_NTOP) for b in range(PEER_NTOP)
               if (a + 1) * (b + 1) <= PEER_NTOP)


def _sorting_network(n):
    pairs = []
    p = 1
    while p < n:
        k = p
        while k >= 1:
            for j in range(k % p, n - k, 2 * k):
                for i in range(min(k, n - j - k)):
                    if (i + j) // (2 * p) == (i + j + k) // (2 * p):
                        pairs.append((i + j, i + j + k))
            k //= 2
        p *= 2
    return tuple(pairs)


def _top_values(x, n):
    n_tiles = x.shape[0] // SUBLANES
    rows = [x[r * SUBLANES:(r + 1) * SUBLANES] for r in range(n_tiles)]
    for a, b in _sorting_network(n_tiles):
        rows[a], rows[b] = jnp.maximum(rows[a], rows[b]), jnp.minimum(rows[a], rows[b])
    out = []
    for r in range(n):
        m = jnp.max(rows[0], axis=0, keepdims=True)
        out.append(m)
        taken = rows[0] == m
        for q in range(min(n_tiles, n - 1 - r)):
            below = rows[q + 1] if q + 1 < n_tiles else NEG_INF
            rows[q] = jnp.where(taken, below, rows[q])
    return out


def _peer_kernel(h_ref, g_ref, wqt_ref, keys_ref, u_ref, vt_ref, vprev_ref, o_ref,
                 xn_ref, s_ref, e1_ref, se2_ref, top_ref, tau_ref, acc_ref, ht_ref, act_ref, *, tt):
    eb = pl.program_id(1)
    nlb = tt // LANES
    n_ck = PEER_EB // PEER_CHUNK
    chunk = lambda ck: slice(ck * PEER_CHUNK, (ck + 1) * PEER_CHUNK)

    @pl.when(eb == 0)
    def _prepare():
        xn = _rms(h_ref[...], g_ref[...])
        xh, xl = _split(xn)
        xn_ref[...] = (xn * (2.0 ** -0.5)).astype(BF16)
        qt = (_dot_nt(wqt_ref[0], xh) + _dot_nt(wqt_ref[0], xl) + _dot_nt(wqt_ref[1], xh))
        for hc in range(2 * PEER_HEADS):
            qh, ql = _split(qt[hc * PEER_NKEYS:(hc + 1) * PEER_NKEYS, :])
            s_ref[hc] = (jnp.dot(keys_ref[0, hc], qh, preferred_element_type=F32)
                         + jnp.dot(keys_ref[0, hc], ql, preferred_element_type=F32)
                         + jnp.dot(keys_ref[1, hc], qh, preferred_element_type=F32))

        def per_lane_block(lb, carry):
            lanes = pl.ds(pl.multiple_of(lb * LANES, LANES), LANES)
            for hc in range(2 * PEER_HEADS):
                h, c = divmod(hc, 2)
                work = s_ref[hc, :, lanes]
                ex = jnp.exp(work - jnp.max(work, axis=0, keepdims=True))
                if c == 0:
                    e1_ref[h, :, lanes] = ex
                else:
                    for rb in range(PEER_NKEYS // SUBLANES):
                        rows = slice(rb * SUBLANES, (rb + 1) * SUBLANES)
                        se2_ref[lb, rb, 2 * h] = work[rows]
                        se2_ref[lb, rb, 2 * h + 1] = ex[rows]
                for r, m in enumerate(_top_values(work, PEER_NTOP)):
                    top_ref[c, r, pl.ds(h, 1), lanes] = m
            v1 = [top_ref[0, r, :, lanes] for r in range(PEER_NTOP)]
            v2 = [top_ref[1, r, :, lanes] for r in range(PEER_NTOP)]
            cands = [v1[a] + v2[b] for (a, b) in _CANDS]
            cv = []
            for r in range(PEER_TOPK + 1):
                level = list(cands)
                while len(level) > 1:
                    level = [jnp.maximum(level[i], level[i + 1]) if i + 1 < len(level) else level[i]
                             for i in range(0, len(level), 2)]
                m = level[0]
                cv.append(m)
                if r < PEER_TOPK:
                    cands = [jnp.where(x == m, NEG_INF, x) for x in cands]
            z = functools.reduce(lambda a, b: a + b, [jnp.exp(x - cv[0]) for x in cv[:PEER_TOPK]])
            tau_ref[0, :, lanes] = 0.5 * (cv[PEER_TOPK - 1] + cv[PEER_TOPK])
            tau_ref[1, :, lanes] = (2.0 ** -0.5) / z
            return carry

        lax.fori_loop(0, nlb, per_lane_block, 0)
        acc_ref[...] = jnp.zeros_like(acc_ref)
        act_ref[PEER_EB // 2:, :] = jnp.zeros((PEER_EB // 2, tt), BF16)

    irows =pl.ds(pl.multiple_of(eb * PEER_ROWS, PEER_ROWS), PEER_ROWS)
    shape = (SUBLANES, LANES)

    def gate_rows(il, lb):
        lanes = slice(lb * LANES, (lb + 1) * LANES)
        tau = tau_ref[0, :, lanes]
        rz = tau_ref[1, :, lanes]
        theta, e1 = [], []
        for h in range(PEER_HEADS):
            theta.append(jnp.broadcast_to(tau[h:h + 1] - s_ref[2 * h, irows, lanes][il:il + 1], shape))
            e1.append(jnp.broadcast_to(e1_ref[h, irows, lanes][il:il + 1] * rz[h:h + 1], shape))
        for jb in range(PEER_NKEYS // BF16_ROWS):
            pieces = []
            for half in range(2):
                rb = 2 * jb + half
                j0 = rb * SUBLANES
                terms = [jnp.where(se2_ref[lb, rb, 2 * h] >= theta[h],
                                   e1[h] * se2_ref[lb, rb, 2 * h + 1], 0.0) for h in range(PEER_HEADS)]
                while len(terms) > 1:
                    terms = [terms[i] + terms[i + 1] for i in range(0, len(terms), 2)]
                r0 = il * PEER_NKEYS + j0
                hv = ht_ref[r0:r0 + SUBLANES, lanes]
                pieces.append(hv * (1.0 + lax.erf(hv)) * terms[0])
            a0 = il * PEER_NKEYS + jb * BF16_ROWS
            act_ref[a0:a0 + BF16_ROWS, lanes] = jnp.concatenate(pieces, axis=0).astype(BF16)

    lo, hi = slice(0, PEER_EB // 2), slice(PEER_EB // 2, PEER_EB)

    def hval(ck):
        ht_ref[chunk(ck), :] = _dot_nt(u_ref[chunk(ck), :], xn_ref[...])

    def gates(cks):
        for ck in cks:
            for il in range(ck * PEER_CHUNK // PEER_NKEYS, (ck + 1) * PEER_CHUNK // PEER_NKEYS):
                for lb in range(nlb):
                    gate_rows(il, lb)

    first, second = range(n_ck // 2), range(n_ck // 2, n_ck)
    for ck in first:
        hval(ck)
    acc_ref[...] += jnp.dot(vprev_ref[...], act_ref[hi, :], preferred_element_type=F32)
    for ck in second:
        hval(ck)
    gates(first)
    acc_ref[...] += jnp.dot(vt_ref[:, lo], act_ref[lo, :], preferred_element_type=F32)
    gates(second)

    @pl.when(eb == pl.num_programs(1) - 1)
    def _finish():
        tail = jnp.dot(vt_ref[:, hi], act_ref[hi, :], preferred_element_type=F32)
        o_ref[...] = h_ref[...] + (acc_ref[...] + tail).T


def _peer(h, norm_g, wqt, keys, u, vt):
    t_total = h.shape[0]
    tt = min(PEER_TT, t_total)
    assert t_total % tt == 0 and tt % LANES == 0
    n_eb = PEER_EXPERTS // PEER_EB
    n_ck = PEER_EB // PEER_CHUNK
    kern = functools.partial(_peer_kernel, tt=tt)
    return pl.pallas_call(
        kern,
        grid=(t_total // tt, n_eb),
        in_specs=[
            pl.BlockSpec((tt, D_MODEL), lambda t, e: (t, 0)),
            pl.BlockSpec((1, D_MODEL), lambda t, e: (0, 0)),
            pl.BlockSpec((2, PEER_HEADS * PEER_DKEY, D_MODEL), lambda t, e: (0, 0, 0)),
            pl.BlockSpec((2, 2 * PEER_HEADS, PEER_NKEYS, PEER_DKEY // 2), lambda t, e: (0, 0, 0, 0)),
            pl.BlockSpec((PEER_EB, D_MODEL), lambda t, e: (e, 0)),
            pl.BlockSpec((D_MODEL, PEER_EB), lambda t, e: (0, e)),
            pl.BlockSpec((D_MODEL, PEER_EB // 2), lambda t, e: (0, jnp.maximum(2 * e - 1, 0))),
        ],
        out_specs=pl.BlockSpec((tt, D_MODEL), lambda t, e: (t, 0)),
        out_shape=jax.ShapeDtypeStruct((t_total, D_MODEL), F32),
        scratch_shapes=[
            pltpu.VMEM((tt, D_MODEL), BF16),
            pltpu.VMEM((2 * PEER_HEADS, PEER_NKEYS, tt), F32),
            pltpu.VMEM((PEER_HEADS, PEER_NKEYS, tt), F32),
            pltpu.VMEM((tt // LANES, PEER_NKEYS // SUBLANES, 2 * PEER_HEADS, SUBLANES, LANES), F32),
            pltpu.VMEM((2, PEER_NTOP, PEER_HEADS, tt), F32),
            pltpu.VMEM((2, PEER_HEADS, tt), F32),
            pltpu.VMEM((D_MODEL, tt), F32),
            pltpu.VMEM((PEER_EB, tt), F32),
            pltpu.VMEM((PEER_EB, tt), BF16),
        ],
        compiler_params=_cparams("parallel", "arbitrary"),
        name="peer",
    )(h, norm_g, wqt, keys, u, vt, vt)


TOKEN_TILE = 512


def _inproj_kernel(x_ref, g_ref, wr_ref, ww_ref, pr_ref, pw_ref):
    xn = _rms(x_ref[...], g_ref[...]).astype(BF16)
    pr_ref[...] = jnp.dot(xn, wr_ref[...], preferred_element_type=F32)
    pw_ref[...] = jnp.dot(xn, ww_ref[...], preferred_element_type=F32)


def _inproj(x, norm_g, w_ret, w_rw):
    t_total = x.shape[0]
    tm = min(TOKEN_TILE, t_total)
    assert t_total % tm == 0
    const = lambda shape: pl.BlockSpec(shape, lambda t: (0,) * len(shape))
    return pl.pallas_call(
        _inproj_kernel,
        grid=(t_total // tm,),
        in_specs=[pl.BlockSpec((tm, D_MODEL), lambda t: (t, 0)), const((1, D_MODEL)),
                  const((D_MODEL, RET_PROJ)), const((D_MODEL, RWKV_PROJ))],
        out_specs=[pl.BlockSpec((tm, RET_PROJ), lambda t: (t, 0)),
                   pl.BlockSpec((tm, RWKV_PROJ), lambda t: (t, 0))],
        out_shape=[jax.ShapeDtypeStruct((t_total, RET_PROJ), F32),
                   jax.ShapeDtypeStruct((t_total, RWKV_PROJ), F32)],
        compiler_params=_cparams("parallel"),
        name="inproj",
    )(x, norm_g, w_ret, w_rw)


def _rope_partner(x):
    lane = lax.broadcasted_iota(jnp.int32, x.shape, 1)
    up = pltpu.roll(x, x.shape[1] - RET_DK // 2, 1)
    down = pltpu.roll(x, RET_DK // 2, 1)
    return jnp.where(lane % RET_DK < RET_DK // 2, up, down)


def _retention_kernel(p_ref, s0_ref, cos_ref, sin_ref, dmat_ref, xz_ref, ones_ref, o_ref, so_ref, s_ref,
                      *, nb, g_chunk):
    ci = pl.program_id(1)
    qk = RET_HEADS * RET_DK
    n_pairs = RET_HEADS // 2
    row_head = lax.broadcasted_iota(jnp.int32, (LANES, LANES), 0) // RET_DK
    lane_head = lax.broadcasted_iota(jnp.int32, (LANES, LANES), 1) // RET_DV
    same_head = row_head == lane_head

    @pl.when(ci == 0)
    def _init():
        s_ref[...] = jnp.zeros_like(s_ref)
        for p in range(n_pairs):
            s_ref[:, p, :RET_DK, :RET_DV] = s0_ref[:, 2 * p]
            s_ref[:, p, RET_DK:, RET_DV:] = s0_ref[:, 2 * p + 1]

    cos = cos_ref[...]
    sin = sin_ref[...]

    def per_seq(n, carry):
        x = p_ref[n]
        q = x[:, :qk]
        k = x[:, qk:2 * qk]
        q = q * cos + _rope_partner(q) * sin
        k = (k * cos + _rope_partner(k) * sin) * (RET_DK ** -0.5)
        lane_is_hi = lax.broadcasted_iota(jnp.int32, (q.shape[0], LANES), 1) >= RET_DK
        for p in range(n_pairs):
            ls = slice(p * LANES, (p + 1) * LANES)
            qp, kp, vp = q[:, ls], k[:, ls], x[:, 2 * qk + p * LANES:2 * qk + (p + 1) * LANES]
            kb, vb = kp.astype(BF16), vp.astype(BF16)
            s_p = s_ref[n, p]
            o = jnp.dot(qp.astype(BF16), s_p.astype(BF16), preferred_element_type=F32) * xz_ref[0, p]
            for hh in range(2):
                mine = lane_is_hi if hh else jnp.logical_not(lane_is_hi)
                inner = _dot_nt(jnp.where(mine, qp, 0.0).astype(BF16), kb) * dmat_ref[2 * p + hh]
                o = o + jnp.dot(inner.astype(BF16), jnp.where(mine, vp, 0.0).astype(BF16),
                                preferred_element_type=F32)
            o_ref[n, :, ls] = o
            decay = jnp.where(row_head == 0, g_chunk[2 * p], g_chunk[2 * p + 1])
            grown = _dot_tn((kp * xz_ref[1, p]).astype(BF16), vb)
            s_ref[n, p] = s_p * decay + jnp.where(same_head, grown, 0.0)
        o = o_ref[n]
        g = x[:, 2 * qk + RET_W:]
        oc = o - _head_sums(o, ones_ref[...]) * (1.0 / RET_DV)
        var = _head_sums(oc * oc, ones_ref[...]) * (1.0 / RET_DV)
        o_ref[n] = oc * lax.rsqrt(var + GN_EPS) * (g * jax.nn.sigmoid(g))
        return carry

    lax.fori_loop(0, nb, per_seq, 0, unroll=2)

    @pl.when(ci == pl.num_programs(1) - 1)
    def _done():
        for p in range(n_pairs):
            so_ref[:, 2 * p] = s_ref[:, p, :RET_DK, :RET_DV]
            so_ref[:, 2 * p + 1] = s_ref[:, p, RET_DK:, RET_DV:]


def _retention(proj_ret, s0, pos, head_ones):
    assert RET_W == RWKV_W and RET_DV == RWKV_N
    b, l, _ = proj_ret.shape
    c = RET_CHUNK if l % RET_CHUNK == 0 else l
    nb = 2 if c == RET_CHUNK else 16
    assert b % nb == 0
    half = RET_DK // 2
    inv = ROPE_BASE ** (-jnp.arange(half, dtype=F32) / half)
    ang = pos.astype(F32)[:, None] * inv[None, :]
    cos = jnp.tile(jnp.cos(ang), (1, 2 * RET_HEADS))
    sin = jnp.tile(jnp.concatenate([-jnp.sin(ang), jnp.sin(ang)], axis=1), (1, RET_HEADS))
    lg = jnp.log(1.0 - 2.0 ** (-5.0 - jnp.arange(RET_HEADS, dtype=F32)))
    idx = jnp.arange(c, dtype=F32)
    diff = idx[:, None] - idx[None, :]
    dmat = jnp.where(diff[None] >= 0, jnp.exp(jnp.maximum(diff, 0.0)[None] * lg[:, None, None]), 0.0)
    xi = jnp.exp((idx + 1.0)[None, :] * lg[:, None])
    zeta = jnp.exp((c - 1.0 - idx)[None, :] * lg[:, None])
    pair_lanes = lambda a: jnp.repeat(a.reshape(RET_HEADS // 2, 2, c).transpose(0, 2, 1), RET_DV, axis=-1)
    xz = jnp.stack([pair_lanes(xi), pair_lanes(zeta)])
    g_chunk = tuple(math.exp(c * math.log(1.0 - 2.0 ** (-5.0 - h))) for h in range(RET_HEADS))
    kern = functools.partial(_retention_kernel, nb=nb, g_chunk=g_chunk)
    return pl.pallas_call(
        kern,
        grid=(b // nb, l // c),
        in_specs=[
            pl.BlockSpec((nb, c, RET_PROJ), lambda i, j: (i, j, 0)),
            pl.BlockSpec((nb, RET_HEADS, RET_DK, RET_DV), lambda i, j: (i, 0, 0, 0)),
            pl.BlockSpec((c, RET_HEADS * RET_DK), lambda i, j: (j, 0)),
            pl.BlockSpec((c, RET_HEADS * RET_DK), lambda i, j: (j, 0)),
            pl.BlockSpec((RET_HEADS, c, c), lambda i, j: (0, 0, 0)),
            pl.BlockSpec((2, RET_HEADS // 2, c, LANES), lambda i, j: (0, 0, 0, 0)),
            pl.BlockSpec((RET_W, RET_W), lambda i, j: (0, 0)),
        ],
        out_specs=[pl.BlockSpec((nb, c, RET_W), lambda i, j: (i, j, 0)),
                   pl.BlockSpec((nb, RET_HEADS, RET_DK, RET_DV), lambda i, j: (i, 0, 0, 0))],
        out_shape=[jax.ShapeDtypeStruct((b, l, RET_W), F32),
                   jax.ShapeDtypeStruct((b, RET_HEADS, RET_DK, RET_DV), F32)],
        scratch_shapes=[pltpu.VMEM((nb, RET_HEADS // 2, 2 * RET_DK, 2 * RET_DV), F32)],
        compiler_params=_cparams("parallel", "arbitrary"),
        name="retention",
    )(proj_ret, s0, cos, sin, dmat, xz, head_ones)


def _softplus(x):
    return jnp.maximum(x, 0.0) + jnp.log1p(jnp.exp(-jnp.abs(x)))


def _rwkv_prep_kernel(x_ref, sh_ref, mu_ref, w0_ref, w2_ref, a0_ref, a2_ref, g2_ref, kks_ref, ka_ref,
                      rk_ref, ones_ref, kkw_ref, cck_ref, rv_ref, gate_ref, bonus_ref,
                      carry_ref, *, nb, tb):
    ti = pl.program_id(1)
    x = x_ref[...]
    first = jnp.where(ti == 0, sh_ref[...], carry_ref[...])
    prev = pltpu.roll(x.reshape(nb * tb, RWKV_PROJ), 1, 0).reshape(nb, tb, RWKV_PROJ)
    tpos = lax.broadcasted_iota(jnp.int32, x.shape, 1)
    prev = jnp.where(tpos == 0, first, prev)
    carry_ref[...] = x[:, tb - 1:tb, :]
    f = (x + (prev - x) * mu_ref[...]).reshape(nb * tb, RWKV_PROJ)
    o1, o2, o3 = RWKV_W, 2 * RWKV_W, 3 * RWKV_W
    r, k7, v7 = f[:, :o1], f[:, o1:o2], f[:, o2:o3]
    fw = f[:, o3:o3 + DECAY_LORA]
    fa = f[:, o3 + DECAY_LORA:o3 + DECAY_LORA + AAA_LORA]
    fg = f[:, o3 + DECAY_LORA + AAA_LORA:]
    w = -_softplus(-(w0_ref[...] + _dot3(jnp.tanh(fw), w2_ref[...]))) - 0.5
    a = jax.nn.sigmoid(a0_ref[...] + _dot3(fa, a2_ref[...]))
    gate = _dot3(jax.nn.sigmoid(fg), g2_ref[...])
    kk = k7 * kks_ref[...]
    kk = kk / jnp.maximum(jnp.sqrt(_head_sums(kk * kk, ones_ref[...])), 1e-12)
    k = k7 * (1.0 + (a - 1.0) * ka_ref[...])
    out3 = lambda y: y.reshape(nb, tb, RWKV_W)
    kkw_ref[:, :, :RWKV_W] = out3(kk)
    kkw_ref[:, :, RWKV_W:] = out3(jnp.exp(-jnp.exp(w)))
    cck_ref[:, :, :RWKV_W] = out3(kk * a)
    cck_ref[:, :, RWKV_W:] = out3(k)
    rv_ref[:, :, :RWKV_W] = out3(r)
    rv_ref[:, :, RWKV_W:] = out3(v7)
    gate_ref[...] = out3(gate)
    bonus_ref[...] = out3(_head_sums(r * k * rk_ref[...], ones_ref[...]) * v7)


def _rwkv_prep(proj_rw, shift, wt):
    b, l, _ = proj_rw.shape
    tb = min(TOKEN_TILE, l)
    nb = max(1, 128 // tb)
    assert l % tb == 0 and b % nb == 0 and tb % SUBLANES == 0
    const = lambda shape: pl.BlockSpec(shape, lambda i, j: (0,) * len(shape))
    blk = pl.BlockSpec((nb, tb, RWKV_W), lambda i, j: (i, j, 0))
    blk2 = pl.BlockSpec((nb, tb, 2 * RWKV_W), lambda i, j: (i, j, 0))
    kern = functools.partial(_rwkv_prep_kernel, nb=nb, tb=tb)
    return pl.pallas_call(
        kern,
        grid=(b // nb, l // tb),
        in_specs=[
            pl.BlockSpec((nb, tb, RWKV_PROJ), lambda i, j: (i, j, 0)),
            pl.BlockSpec((nb, 1, RWKV_PROJ), lambda i, j: (i, 0, 0)),
            const((1, RWKV_PROJ)), const((1, RWKV_W)), const((DECAY_LORA, RWKV_W)),
            const((1, RWKV_W)), const((AAA_LORA, RWKV_W)), const((GATE_LORA, RWKV_W)),
            const((1, RWKV_W)), const((1, RWKV_W)), const((1, RWKV_W)), const((RWKV_W, RWKV_W)),
        ],
        out_specs=[blk2] * 3 + [blk] * 2,
        out_shape=[jax.ShapeDtypeStruct((b, l, 2 * RWKV_W), F32)] * 3 + [jax.ShapeDtypeStruct((b, l, RWKV_W), F32)] * 2,
        scratch_shapes=[pltpu.VMEM((nb, 1, RWKV_PROJ), F32)],
        compiler_params=_cparams("parallel", "arbitrary"),
        name="rwkv_prep",
    )(proj_rw, shift.reshape(b, 1, RWKV_PROJ), wt['mu'], wt['w0'], wt['w2'], wt['a0'], wt['a2'],
      wt['g2'], wt['kk'], wt['ka'], wt['rk'], wt['head_ones'])


def _wkv_kernel(*refs, tb, ni, paired):
    n_in = 3 if paired else 6
    in_refs, (s0_ref, y_ref, so_ref, s_ref, j_ref, v_ref) = refs[:n_in], refs[n_in:]
    ti = pl.program_id(1)

    @pl.when(ti == 0)
    def _init():
        s_ref[...] = s0_ref[0]

    if paired:
        def spread(t, carry):
            low = lax.broadcasted_iota(jnp.int32, (RWKV_N, LANES), 1) < LANES // 2
            low_v = lax.broadcasted_iota(jnp.int32, (ni, LANES), 1) < LANES // 2
            kkw, cck, rv = (ref[0, t] for ref in in_refs)
            kkw_sw, cck_sw, rv_sw = (pltpu.roll(x, LANES // 2, 1) for x in (kkw, cck, rv))
            j_ref[t, 0], j_ref[t, 1] = jnp.where(low, kkw, kkw_sw), jnp.where(low, kkw_sw, kkw)
            j_ref[t, 2], j_ref[t, 3] = jnp.where(low, cck, cck_sw), jnp.where(low, cck_sw, cck)
            j_ref[t, 4] = jnp.where(low, rv, rv_sw)
            v_ref[t] = jnp.where(low_v, rv_sw[:ni], rv[ni:])
            return carry

        lax.fori_loop(0, tb, spread, 0, unroll=8)

    def step(t, carry):
        if paired:
            kk, w, cc, k, r = (j_ref[t, i] for i in range(5))
            vrow = lambda ii: v_ref[t, ii:ii + 1, :]
        else:
            kk, w, cc, k, r = (ref[0, t] for ref in in_refs[:5])
            vrow = lambda ii: in_refs[5][0, t, ii:ii + 1, :]
        for ii in range(ni):
            s = s_ref[ii]
            sa = -jnp.sum(s * kk, axis=0, keepdims=True)
            sn = s * w + sa * cc + vrow(ii) * k
            s_ref[ii] = sn
            y_ref[0, t, ii:ii + 1, :] = jnp.sum(sn * r, axis=0, keepdims=True)
        return carry

    lax.fori_loop(0, tb, step, 0)

    @pl.when(ti == pl.num_programs(1) - 1)
    def _done():
        so_ref[0] = s_ref[...]


def _wkv_scan(inputs, s0):
    paired = len(inputs) == 3
    g, ni = s0.shape[:2]
    l = inputs[0].shape[1]
    tb = min(64, l)
    assert l % tb == 0 and (not paired or 2 * ni == RWKV_N)
    jblk = pl.BlockSpec((1, tb, RWKV_N, LANES), lambda i, j: (i, j, 0, 0))
    iblk = pl.BlockSpec((1, tb, ni, LANES), lambda i, j: (i, j, 0, 0))
    sblk = pl.BlockSpec((1, ni, RWKV_N, LANES), lambda i, j: (i, 0, 0, 0))
    kern = functools.partial(_wkv_kernel, tb=tb, ni=ni, paired=paired)
    return pl.pallas_call(
        kern,
        grid=(g, l // tb),
        in_specs=([jblk] * 3 if paired else [jblk] * 5 + [iblk]) + [sblk],
        out_specs=[iblk, sblk],
        out_shape=[jax.ShapeDtypeStruct((g, l, ni, LANES), F32),
                   jax.ShapeDtypeStruct((g, ni, RWKV_N, LANES), F32)],
        scratch_shapes=[pltpu.VMEM((ni, RWKV_N, LANES), F32),
                        pltpu.VMEM((tb if paired else 1, 5, RWKV_N, LANES), F32),
                        pltpu.VMEM((tb if paired else 1, ni, LANES), F32)],
        compiler_params=_cparams("parallel", "arbitrary"),
        name="wkv_scan",
    )(*inputs, s0)


def _outproj_kernel(x_ref, oret_ref, y_ref, bonus_ref, gate_ref, lng_ref, lnb_ref, ones_ref,
                    wtop_ref, wbot_ref, o_ref):
    y = y_ref[...]
    yc = y - _head_sums(y, ones_ref[...]) * (1.0 / RWKV_N)
    var = _head_sums(yc * yc, ones_ref[...]) * (1.0 / RWKV_N)
    o = yc * lax.rsqrt(var + RWKV_LN_EPS) * lng_ref[...] + lnb_ref[...]
    o = (o + bonus_ref[...]) * gate_ref[...]
    o_ref[...] = (x_ref[...]
                  + jnp.dot(oret_ref[...].astype(BF16), wtop_ref[...], preferred_element_type=F32)
                  + jnp.dot(o.astype(BF16), wbot_ref[...], preferred_element_type=F32))


def _outproj(x, o_ret, y, bonus, gate, wt):
    t_total = x.shape[0]
    tm = min(TOKEN_TILE, t_total)
    const = lambda shape: pl.BlockSpec(shape, lambda t: (0,) * len(shape))
    half = pl.BlockSpec((tm, RWKV_W), lambda t: (t, 0))
    full = pl.BlockSpec((tm, D_MODEL), lambda t: (t, 0))
    return pl.pallas_call(
        _outproj_kernel,
        grid=(t_total // tm,),
        in_specs=[full, half, half, half, half, const((1, RWKV_W)), const((1, RWKV_W)),
                  const((RWKV_W, RWKV_W)), const((RET_W, D_MODEL)), const((RWKV_W, D_MODEL))],
        out_specs=full,
        out_shape=jax.ShapeDtypeStruct((t_total, D_MODEL), F32),
        compiler_params=_cparams("parallel"),
        name="outproj",
    )(x, o_ret, y, bonus, gate, wt['ln_g'], wt['ln_b'], wt['head_ones'], wt['w_out_top'], wt['w_out_bot'])


def _ple_kernel(h_ref, p_ref, gn_ref, wg_ref, wp_ref, fn_ref, o_ref):
    h = h_ref[...]
    gate = jax.nn.sigmoid(jnp.dot(_rms(h, gn_ref[...]).astype(BF16), wg_ref[...], preferred_element_type=F32))
    h = h + gate * jnp.dot(p_ref[...].astype(BF16), wp_ref[...], preferred_element_type=F32)
    o_ref[...] = _rms(h, fn_ref[...])


def _ple(h, p, wt):
    t_total = h.shape[0]
    tm = min(TOKEN_TILE, t_total)
    const = lambda shape: pl.BlockSpec(shape, lambda t: (0,) * len(shape))
    full = pl.BlockSpec((tm, D_MODEL), lambda t: (t, 0))
    return pl.pallas_call(
        _ple_kernel,
        grid=(t_total // tm,),
        in_specs=[full, pl.BlockSpec((tm, PLE_DIM), lambda t: (t, 0)), const((1, D_MODEL)),
                  const((D_MODEL, D_MODEL)), const((PLE_DIM, D_MODEL)), const((1, D_MODEL))],
        out_specs=full,
        out_shape=jax.ShapeDtypeStruct((t_total, D_MODEL), F32),
        compiler_params=_cparams("parallel"),
        name="ple",
    )(h, p, wt['ple_norm'], wt['ple_gate'], wt['ple_proj'], wt['norm_final'])


def _prep_weights(p):
    row = lambda a: a.reshape(1, -1)
    lane_head = jnp.arange(RWKV_W) // RWKV_N
    return {
        'norm_mix': row(p['norm_mix'][0]),
        'w_ret': p['w_in'][0][:, :RET_PROJ].astype(BF16),
        'w_rw': p['w_in'][0][:, RET_PROJ:].astype(BF16),
        'mu': row(p['rw_mu'][0]), 'w0': row(p['rw_w0'][0]), 'w2': p['rw_w2'][0],
        'a0': row(p['rw_a0'][0]), 'a2': p['rw_a2'][0], 'g2': p['rw_g2'][0],
        'kk': row(p['rw_kk'][0]), 'ka': row(p['rw_ka'][0]), 'rk': row(p['rw_rk'][0]),
        'ln_g': row(p['rw_ln_g'][0]), 'ln_b': row(p['rw_ln_b'][0]),
        'head_ones': (lane_head[:, None] == lane_head[None, :]).astype(BF16),
        'w_out_top': p['w_out'][0][:RET_W].astype(BF16),
        'w_out_bot': p['w_out'][0][RET_W:].astype(BF16),
        'norm_ffn': row(p['norm_ffn'][0]),
        'wqt': jnp.stack(_split(p['peer_wq'][0].T)),
        'keys': jnp.stack(_split(p['peer_keys'][0].reshape(2 * PEER_HEADS, PEER_NKEYS, PEER_DKEY // 2))),
        'u': p['peer_u'][0].astype(BF16),
        'vt': p['peer_v'][0].astype(BF16).T,
        'ple_norm': row(p['ple_norm'][0]),
        'ple_gate': p['ple_gate'][0].astype(BF16),
        'ple_proj': p['ple_proj'][0].astype(BF16),
        'norm_final': row(p['norm_final']),
    }


def _scan_inputs(pairs, b):
    l = pairs[0].shape[1]
    if b * RWKV_HEADS * 2 == LANES:
        return [p.reshape(b, l, 2, RWKV_HEADS, RWKV_N).transpose(1, 4, 2, 0, 3).reshape(1, l, RWKV_N, LANES)
                for p in pairs]
    assert b == LANES
    split = lambda p: [h.reshape(b, l, RWKV_HEADS, RWKV_N).transpose(2, 1, 3, 0)
                       for h in (p[..., :RWKV_W], p[..., RWKV_W:])]
    (kk, w), (cc, k), (r, v) = (split(p) for p in pairs)
    return [kk, w, cc, k, r, v]


def _values_from_lanes(y, b):
    g, l, ni, _ = y.shape
    if g == 1:
        y = y.reshape(l, ni, 2, b, RWKV_HEADS).transpose(3, 0, 4, 2, 1)
    else:
        y = y.transpose(3, 1, 0, 2)
    return y.reshape(b, l, RWKV_W)


def _state_to_lanes(s):
    b = s.shape[0]
    if b * RWKV_HEADS * 2 == LANES:
        s = s.reshape(b, RWKV_HEADS, 2, RWKV_N // 2, RWKV_N).transpose(3, 4, 2, 0, 1)
        return s.reshape(1, RWKV_N // 2, RWKV_N, LANES)
    return s.transpose(1, 2, 3, 0)


def _state_from_lanes(s, b):
    g, ni, _, _ = s.shape
    if g == 1:
        s = s.reshape(ni, RWKV_N, 2, b, RWKV_HEADS).transpose(3, 4, 2, 0, 1)
        return s.reshape(b, RWKV_HEADS, RWKV_N, RWKV_N)
    return s.transpose(3, 0, 1, 2)


def _layer(x, p_l, pos, s_ret, s_wkv, s_shift, wt):
    b, l, _ = x.shape
    x2 = x.reshape(b * l, D_MODEL)
    proj_ret, proj_rw = _inproj(x2, wt['norm_mix'], wt['w_ret'], wt['w_rw'])
    o_ret, s_ret_new = _retention(proj_ret.reshape(b, l, RET_PROJ), s_ret, pos, wt['head_ones'])
    proj_rw = proj_rw.reshape(b, l, RWKV_PROJ)
    new_shift = proj_rw[:, -1]
    kkw, cck, rv, gate, bonus = _rwkv_prep(proj_rw, s_shift, wt)
    y, s_lanes = _wkv_scan(_scan_inputs((kkw, cck, rv), b), _state_to_lanes(s_wkv))
    y = _values_from_lanes(y, b)
    s_wkv_new = _state_from_lanes(s_lanes, b)
    flat = lambda a: a.reshape(b * l, -1)
    h = _outproj(x2, flat(o_ret), flat(y), flat(bonus), flat(gate), wt)
    h = _peer(h, wt['norm_ffn'], wt['wqt'], wt['keys'], wt['u'], wt['vt'])
    y_out = _ple(h, p_l.reshape(b * l, PLE_DIM), wt)
    return y_out.reshape(b, l, D_MODEL), s_ret_new, s_wkv_new, new_shift


def kernel(x_prompt, x_sample, state_ret, state_wkv, state_shift, p_prompt, p_sample, norm_mix, w_in, rw_mu, rw_w0, rw_w2, rw_a0, rw_a2, rw_g2, rw_kk, rw_ka, rw_rk, rw_ln_g, rw_ln_b, w_out, norm_ffn, peer_wq, peer_keys, peer_u, peer_v, ple_norm, ple_gate, ple_proj, norm_final):
    assert norm_mix.shape[0] == 1, "single-layer trunk"
    wt = _prep_weights(dict(
        norm_mix=norm_mix, w_in=w_in, rw_mu=rw_mu, rw_w0=rw_w0, rw_w2=rw_w2, rw_a0=rw_a0, rw_a2=rw_a2,
        rw_g2=rw_g2, rw_kk=rw_kk, rw_ka=rw_ka, rw_rk=rw_rk, rw_ln_g=rw_ln_g, rw_ln_b=rw_ln_b, w_out=w_out,
        norm_ffn=norm_ffn, peer_wq=peer_wq, peer_keys=peer_keys, peer_u=peer_u, peer_v=peer_v,
        ple_norm=ple_norm, ple_gate=ple_gate, ple_proj=ple_proj, norm_final=norm_final))
    bp, lp, _ = x_prompt.shape
    bs, ls, _ = x_sample.shape
    pos_p = jnp.arange(lp, dtype=jnp.int32)
    pos_s = PAST_LEN + jnp.arange(ls, dtype=jnp.int32)
    zero_ret = jnp.zeros((bp, RET_HEADS, RET_DK, RET_DV), F32)
    zero_wkv = jnp.zeros((bp, RWKV_HEADS, RWKV_N, RWKV_N), F32)
    zero_shift = jnp.zeros((bp, RWKV_PROJ), F32)
    yp, rp, wp, sp = _layer(x_prompt, p_prompt[0], pos_p, zero_ret, zero_wkv, zero_shift, wt)
    ys, rs, ws, ss = _layer(x_sample, p_sample[0], pos_s, state_ret[0], state_wkv[0], state_shift[0], wt)
    return (yp, ys, rp[None], wp[None], sp[None], rs[None], ws[None], ss[None])
```
